```python
import math, functools
import jax, jax.numpy as jnp
from jax import lax
import numpy as np

D_MODEL = 1024
BATCH = 4
SEQ = 4096
DEPTH = 2
DEC_BATCH = 32
DEC_SEQ = 8
PAST_LEN = 16384
PAGE_SIZE = 128

D_PLE = 256
HD_A = 64
DKV_A = 2 * HD_A
W_A = D_MODEL // 2
H_A = W_A // DKV_A
W_B = D_MODEL - W_A
H_B = 4
DV_B = W_B // H_B
DK_B = DV_B // 2
GATE_RANK = 16
GATE_NORM = 16.0
GLA_CHUNK = 64
IN_SPLITS = (W_A, W_A, W_A, H_B * DK_B, H_B * DK_B, W_B, W_B, GATE_RANK)
IN_WIDTH = 3 * W_A + 2 * H_B * DK_B + 2 * W_B + GATE_RANK
N_GROUPS = 4
EXP_PER_GROUP = 8
N_EXPERTS = N_GROUPS * EXP_PER_GROUP
TOP_K = 2
D_EXPERT = D_MODEL // 2
MOE_BLOCK = 128
ROPE_THETA = 10000.0
Q_BLOCK = 128
ALPHA = (2 * DEPTH) ** 0.25
BETA = (8 * DEPTH) ** -0.25
EPS = 1e-5

kernel_name = 'hymba_diff_gla_hmoe_deepnorm_decode_step'

F32 = jnp.float32


def _layernorm(x, g, b):
    xf = x.astype(F32)
    mu = jnp.mean(xf, -1, keepdims=True)
    var = jnp.mean(jnp.square(xf - mu), -1, keepdims=True)
    return ((xf - mu) * lax.rsqrt(var + EPS) * g + b).astype(x.dtype)


def _rmsnorm(x, g):
    xf = x.astype(F32)
    return (xf * lax.rsqrt(jnp.mean(jnp.square(xf), -1, keepdims=True) + EPS) * g).astype(x.dtype)


def _rope(x, pos):
    half = x.shape[-1] // 2
    inv = ROPE_THETA ** (-jnp.arange(half, dtype=F32) / half)
    ang = pos.astype(F32)[:, None] * inv[None, :]
    cos = jnp.cos(ang)[None, :, None, :]
    sin = jnp.sin(ang)[None, :, None, :]
    xf = x.astype(F32)
    x1, x2 = xf[..., :half], xf[..., half:]
    return jnp.concatenate([x1 * cos - x2 * sin, x2 * cos + x1 * sin], -1).astype(x.dtype)


def _diff_attend(q, ks, vs, masks, lam):
    B, Tq, H, _ = q.shape
    qc = q.reshape(B, Tq, H, 2, HD_A)
    s = jnp.concatenate(
        [jnp.einsum('bqhcd,bkhcd->cbhqk', qc, k.reshape(k.shape[:3] + (2, HD_A)),
                    preferred_element_type=F32) for k in ks], -1) * (HD_A ** -0.5)
    m = jnp.concatenate(masks, -1)
    p = jax.nn.softmax(jnp.where(m, s, -jnp.inf), axis=-1)
    pd = p[0] - lam * p[1]
    out = None
    off = 0
    for v in vs:
        n = v.shape[1]
        o = jnp.einsum('bhqk,bkhd->bqhd', pd[..., off:off + n].astype(v.dtype), v)
        out = o if out is None else out + o
        off += n
    return out


def _attend_prompt(qa, ka, va, lam):
    B, T, H, D = qa.shape
    nb = T // Q_BLOCK
    kpos = jnp.arange(T)
    qb = qa.reshape(B, nb, Q_BLOCK, H, D).transpose(1, 0, 2, 3, 4)
    qpos = kpos.reshape(nb, Q_BLOCK)

    def blk(args):
        q, qp = args
        return _diff_attend(q, [ka], [va], [qp[:, None] >= kpos[None, :]], lam)

    o = lax.map(blk, (qb, qpos))
    return o.transpose(1, 0, 2, 3, 4).reshape(B, T, H, D)


def _attend_sample(qa, ka, va, lam, k_past, v_past):
    T = qa.shape[1]
    P = k_past.shape[1]
    m_past = jnp.ones((T, P), bool)
    m_new = jnp.tril(jnp.ones((T, T), bool))
    return _diff_attend(qa, [k_past, ka], [v_past, va], [m_past, m_new], lam)


def _gla(q, k, v, logg, S0, chunk):
    B, T, H, DK = q.shape
    DV = v.shape[-1]
    n = T // chunk

    def blocks(a):
        return a.astype(F32).reshape(B, n, chunk, H, a.shape[-1]).transpose(1, 0, 3, 2, 4)

    qc = blocks(q) * (DK ** -0.5)
    kc = blocks(k)
    vc = blocks(v)
    bc = jnp.cumsum(blocks(logg), axis=-2)
    causal = jnp.tril(jnp.ones((chunk, chunk), bool))

    def step(S, inp):
        qi, ki, vi, bi = inp
        bl = bi[:, :, -1:, :]
        qg = qi * jnp.exp(bi)
        kg = ki * jnp.exp(-bi)
        a = jnp.where(causal, jnp.einsum('bhid,bhjd->bhij', qg, kg), 0.0)
        o = jnp.einsum('bhid,bhde->bhie', qg, S) + jnp.einsum('bhij,bhje->bhie', a, vi)
        S = jnp.exp(bl[:, :, 0, :, None]) * S + jnp.einsum('bhjd,bhje->bhde', ki * jnp.exp(bl - bi), vi)
        return S, o

    S, o = lax.scan(step, S0.astype(F32), (qc, kc, vc, bc))
    o = o.transpose(1, 0, 3, 2, 4).reshape(B, T, H, DV)
    return o.astype(v.dtype), S


def _hier_moe(x, w_r1, b_r1, w_r2, b_r2, w_gate, w_up, w_down):
    B, T, D = x.shape
    xf = x.reshape(-1, D)
    N = xf.shape[0]
    lg1 = (xf @ w_r1 + b_r1).astype(F32)
    p1 = jax.nn.softmax(lg1, -1)
    g_sel = jnp.argmax(lg1, -1)
    pg = jnp.take_along_axis(p1, g_sel[:, None], -1)
    lg2 = (xf @ w_r2 + b_r2).astype(F32).reshape(N, N_GROUPS, EXP_PER_GROUP)
    lg2 = jnp.take_along_axis(lg2, g_sel[:, None, None], 1)[:, 0]
    top_v, top_i = lax.top_k(lg2, TOP_K)
    gates = pg * jax.nn.softmax(top_v, -1)
    eid = g_sel[:, None] * EXP_PER_GROUP + top_i
    A = N * TOP_K
    e_flat = eid.reshape(-1)
    t_flat = jnp.repeat(jnp.arange(N), TOP_K)
    w_flat = gates.reshape(-1)
    order = jnp.argsort(e_flat, stable=True)
    e_s, t_s, w_s = e_flat[order], t_flat[order], w_flat[order]
    counts = jnp.bincount(e_flat, length=N_EXPERTS)
    starts = jnp.cumsum(counts) - counts
    padded = (counts + MOE_BLOCK - 1) // MOE_BLOCK * MOE_BLOCK
    pends = jnp.cumsum(padded)
    pstarts = pends - padded
    dest = pstarts[e_s] + jnp.arange(A) - starts[e_s]
    n_blocks = (A + MOE_BLOCK - 1) // MOE_BLOCK + N_EXPERTS
    R = n_blocks * MOE_BLOCK
    xbuf = jnp.zeros((R, D), x.dtype).at[dest].set(xf[t_s])
    block_e = jnp.minimum(jnp.searchsorted(pends, jnp.arange(n_blocks) * MOE_BLOCK, side='right'),
                          N_EXPERTS - 1)

    def expert_block(args):
        xb, e = args
        h = jax.nn.silu(xb @ w_gate[e]) * (xb @ w_up[e])
        return h @ w_down[e]

    ybuf = lax.map(expert_block, (xbuf.reshape(n_blocks, MOE_BLOCK, D), block_e)).reshape(R, D)
    y = jnp.zeros((N, D), F32).at[t_s].add(w_s[:, None] * ybuf[dest].astype(F32))
    return y.reshape(B, T, D).astype(x.dtype)


def _layer(x, p_l, pos, li, attend_a, S0, chunk,
           w_in, w_gk2, b_gk2, lam_q1, lam_k1, lam_q2, lam_k2, diff_g, gla_g, w_o,
           ln1_g, ln1_b, w_r1, b_r1, w_r2, b_r2, w_gate, w_up, w_down, ln2_g, ln2_b,
           w_pg, w_pp, ple_g):
    B, T, _ = x.shape
    h = x @ w_in
    qa, ka, va, qb, kb, vb, gb, glr = jnp.split(h, np.cumsum(IN_SPLITS)[:-1].tolist(), axis=-1)
    qa = _rope(qa.reshape(B, T, 2 * H_A, HD_A), pos).reshape(B, T, H_A, DKV_A)
    ka = _rope(ka.reshape(B, T, 2 * H_A, HD_A), pos).reshape(B, T, H_A, DKV_A)
    va = va.reshape(B, T, H_A, DKV_A)
    lam_init = 0.8 - 0.6 * math.exp(-0.3 * li)
    lam = (jnp.exp(jnp.sum(lam_q1.astype(F32) * lam_k1)) - jnp.exp(jnp.sum(lam_q2.astype(F32) * lam_k2))
           + lam_init)
    oa = attend_a(qa, ka, va, lam)
    logg = jax.nn.log_sigmoid((glr @ w_gk2 + b_gk2).astype(F32)).reshape(B, T, H_B, DK_B) / GATE_NORM
    ob, S = _gla(qb.reshape(B, T, H_B, DK_B), kb.reshape(B, T, H_B, DK_B),
                 vb.reshape(B, T, H_B, DV_B), logg, S0, chunk)
    oa = _rmsnorm(oa, diff_g) * (1.0 - lam_init)
    ob = _rmsnorm(ob, gla_g).astype(x.dtype) * jax.nn.silu(gb.reshape(B, T, H_B, DV_B))
    mix = jnp.concatenate([oa.reshape(B, T, W_A), ob.reshape(B, T, W_B)], -1) @ w_o
    x = _layernorm(ALPHA * x + mix, ln1_g, ln1_b)
    x = _layernorm(ALPHA * x + _hier_moe(x, w_r1, b_r1, w_r2, b_r2, w_gate, w_up, w_down), ln2_g, ln2_b)
    e = _rmsnorm(p_l @ w_pp, ple_g)
    x = x + jax.nn.sigmoid(x @ w_pg) * e
    return x, ka, va, S


def setup_inputs(seed: int = 0) -> dict:
    key = jax.random.key(seed)
    k = jax.random.split(key, 40)

    def nrm(i, shape, s):
        return jax.random.normal(k[i], shape, F32) * s

    n_pages = PAST_LEN // PAGE_SIZE
    n_phys = (5 * DEC_BATCH * n_pages + 3) // 4
    page_table = jax.random.permutation(k[0], n_phys)[:DEC_BATCH * n_pages].reshape(
        DEC_BATCH, n_pages).astype(jnp.int32)
    col_scale = jnp.concatenate([jnp.full((w,), BETA if j in (2, 5) else 1.0, F32)
                                 for j, w in enumerate(IN_SPLITS)])
    return {
        'x_prompt': nrm(1, (BATCH, SEQ, D_MODEL), 1.0),
        'x_sample': nrm(2, (DEC_BATCH, DEC_SEQ, D_MODEL), 1.0),
        'cache_k': nrm(3, (n_phys, DEPTH, PAGE_SIZE, H_A, DKV_A), 1.0),
        'cache_v': nrm(4, (n_phys, DEPTH, PAGE_SIZE, H_A, DKV_A), BETA),
        'state_gla': nrm(5, (DEC_BATCH, DEPTH, H_B, DK_B, DV_B), 1.0),
        'page_table': page_table,
        'p_prompt': nrm(6, (DEPTH, BATCH, SEQ, D_PLE), 1.0),
        'p_sample': nrm(7, (DEPTH, DEC_BATCH, DEC_SEQ, D_PLE), 1.0),
        'w_in': nrm(8, (DEPTH, D_MODEL, IN_WIDTH), D_MODEL ** -0.5) * col_scale,
        'w_gk2': nrm(9, (DEPTH, GATE_RANK, H_B * DK_B), GATE_RANK ** -0.5),
        'b_gk2': nrm(10, (DEPTH, H_B * DK_B), 0.02),
        'lam_q1': nrm(11, (DEPTH, HD_A), 0.1),
        'lam_k1': nrm(12, (DEPTH, HD_A), 0.1),
        'lam_q2': nrm(13, (DEPTH, HD_A), 0.1),
        'lam_k2': nrm(14, (DEPTH, HD_A), 0.1),
        'diff_norm_g': 1.0 + nrm(15, (DEPTH, DKV_A), 0.02),
        'gla_norm_g': 1.0 + nrm(16, (DEPTH, DV_B), 0.02),
        'w_o': nrm(17, (DEPTH, D_MODEL, D_MODEL), BETA * D_MODEL ** -0.5),
        'ln1_g': 1.0 + nrm(18, (DEPTH, D_MODEL), 0.02),
        'ln1_b': nrm(19, (DEPTH, D_MODEL), 0.02),
        'w_r1': nrm(20, (DEPTH, D_MODEL, N_GROUPS), D_MODEL ** -0.5),
        'b_r1': nrm(21, (DEPTH, N_GROUPS), 0.01),
        'w_r2': nrm(22, (DEPTH, D_MODEL, N_EXPERTS), D_MODEL ** -0.5),
        'b_r2': nrm(23, (DEPTH, N_EXPERTS), 0.01),
        'w_gate': nrm(24, (DEPTH, N_EXPERTS, D_MODEL, D_EXPERT), D_MODEL ** -0.5),
        'w_up': nrm(25, (DEPTH, N_EXPERTS, D_MODEL, D_EXPERT), D_MODEL ** -0.5),
        'w_down': nrm(26, (DEPTH, N_EXPERTS, D_EXPERT, D_MODEL), BETA * D_EXPERT ** -0.5),
        'ln2_g': 1.0 + nrm(27, (DEPTH, D_MODEL), 0.02),
        'ln2_b': nrm(28, (DEPTH, D_MODEL), 0.02),
        'w_ple_gate': nrm(29, (DEPTH, D_MODEL, D_MODEL), D_MODEL ** -0.5),
        'w_ple_proj': nrm(30, (DEPTH, D_PLE, D_MODEL), D_PLE ** -0.5),
        'ple_norm_g': 1.0 + nrm(31, (DEPTH, D_MODEL), 0.02),
    }


def reference(x_prompt, x_sample, cache_k, cache_v, state_gla, page_table, p_prompt, p_sample,
              w_in, w_gk2, b_gk2, lam_q1, lam_k1, lam_q2, lam_k2, diff_norm_g, gla_norm_g, w_o,
              ln1_g, ln1_b, w_r1, b_r1, w_r2, b_r2, w_gate, w_up, w_down, ln2_g, ln2_b,
              w_ple_gate, w_ple_proj, ple_norm_g):
    Bp, Tp, _ = x_prompt.shape
    Bs, Ts, _ = x_sample.shape
    past = page_table.shape[1] * PAGE_SIZE
    pos_p = jnp.arange(Tp)
    pos_s = past + jnp.arange(Ts)
    S0_p = jnp.zeros((Bp, H_B, DK_B, DV_B), F32)
    stacked = (w_in, w_gk2, b_gk2, lam_q1, lam_k1, lam_q2, lam_k2, diff_norm_g, gla_norm_g, w_o,
               ln1_g, ln1_b, w_r1, b_r1, w_r2, b_r2, w_gate, w_up, w_down, ln2_g, ln2_b,
               w_ple_gate, w_ple_proj, ple_norm_g)
    yp, ys = x_prompt, x_sample
    kp, vp, sp, ks, vs, ss = [], [], [], [], [], []
    for li in range(DEPTH):
        lw = tuple(w[li] for w in stacked)
        yp, ka, va, S = _layer(yp, p_prompt[li], pos_p, li, _attend_prompt, S0_p, GLA_CHUNK, *lw)
        kp.append(ka)
        vp.append(va)
        sp.append(S.astype(x_prompt.dtype))
        k_past = cache_k[page_table, li].reshape(Bs, past, H_A, DKV_A)
        v_past = cache_v[page_table, li].reshape(Bs, past, H_A, DKV_A)
        attend_s = functools.partial(_attend_sample, k_past=k_past, v_past=v_past)
        ys, ka, va, S = _layer(ys, p_sample[li], pos_s, li, attend_s, state_gla[:, li], Ts, *lw)
        ks.append(ka)
        vs.append(va)
        ss.append(S.astype(x_sample.dtype))
    return (yp, ys, jnp.stack(kp, 1), jnp.stack(vp, 1), jnp.stack(sp, 1),
            jnp.stack(ks, 1), jnp.stack(vs, 1), jnp.stack(ss, 1))
```

```python
import functools
import math

import jax
import jax.numpy as jnp
from jax import lax
from jax.experimental import pallas as pl
from jax.experimental.pallas import tpu as pltpu

F32 = jnp.float32
BF16 = jnp.bfloat16
I32 = jnp.int32
HI = lax.Precision.HIGHEST

D_MODEL = 1024
DEPTH = 2
PAGE_SIZE = 128
D_PLE = 256
HD_A = 64
DKV_A = 2 * HD_A
W_A = D_MODEL // 2
H_A = W_A // DKV_A
W_B = D_MODEL - W_A
H_B = 4
DV_B = W_B // H_B
DK_B = DV_B // 2
GATE_RANK = 16
GATE_NORM = 16.0
GLA_CHUNK = 64
N_GROUPS = 4
EXP_PER_GROUP = 8
N_EXPERTS = N_GROUPS * EXP_PER_GROUP
D_EXPERT = D_MODEL // 2
ROPE_THETA = 10000.0
ALPHA = (2 * DEPTH) ** 0.25
EPS = 1e-5

LANES = 128
VMEM_LIMIT = 48 * 1024 * 1024

TOKEN_TILE = 256
ATTN_TILE = 256
PAGES_PER_STEP = 8
GLA_TILE = 256
EXPERT_BLOCK = 256

_OFF_QA, _OFF_KA, _OFF_VA = 0, W_A, 2 * W_A
_OFF_QB = 3 * W_A
_OFF_KB = _OFF_QB + H_B * DK_B
_OFF_VB = _OFF_KB + H_B * DK_B
_OFF_GB = _OFF_VB + W_B
_OFF_GLR = _OFF_GB + W_B

_NT = (((1,), (1,)), ((), ()))
_TN = (((0,), (0,)), ((), ()))


def _cparams(*sem):
    return pltpu.CompilerParams(dimension_semantics=sem, vmem_limit_bytes=VMEM_LIMIT)


def _dot(a, b, precision=None):
    return jnp.dot(a, b, preferred_element_type=F32, precision=precision)


def _lam_init(li):
    return 0.8 - 0.6 * math.exp(-0.3 * li)


def _lam(lamq_ref, lamk_ref, li):
    s = jnp.sum(lamq_ref[...] * lamk_ref[...], axis=1, keepdims=True)
    e = jnp.exp(s)
    return e[0:1, :] - e[1:2, :] + _lam_init(li)


def _rms(x, g):
    return x * lax.rsqrt(jnp.mean(x * x, axis=-1, keepdims=True) + EPS) * g


def _layernorm(x, g, b):
    mu = jnp.mean(x, axis=-1, keepdims=True)
    xc = x - mu
    var = jnp.mean(xc * xc, axis=-1, keepdims=True)
    return xc * lax.rsqrt(var + EPS) * g + b


def _inproj_kernel(x_ref, w_ref, wglr_ref, wgk2_ref, bgk2_ref, cos_ref, sin_ref,
                   qa_ref, ka_ref, va_ref, qb_ref, kb_ref, vb_ref, gb_ref, lg_ref):
    xb = x_ref[...].astype(BF16)
    cos = cos_ref[...]
    sin = sin_ref[...]
    lane = lax.broadcasted_iota(I32, cos.shape, 1)
    first_half = (lane & (HD_A // 2)) == 0

    def rope(h):
        partner = jnp.where(first_half, pltpu.roll(h, LANES - HD_A // 2, 1), pltpu.roll(h, HD_A // 2, 1))
        return h * cos + partner * sin

    for c in range(W_A // LANES):
        lo = c * LANES
        hq = _dot(xb, w_ref[:, _OFF_QA + lo:_OFF_QA + lo + LANES])
        qa_ref[:, lo:lo + LANES] = rope(hq) * (HD_A ** -0.5)
        hk = _dot(xb, w_ref[:, _OFF_KA + lo:_OFF_KA + lo + LANES])
        ka_ref[:, lo:lo + LANES] = rope(hk)
    va_ref[...] = _dot(xb, w_ref[:, _OFF_VA:_OFF_VA + W_A])
    qb_ref[...] = _dot(xb, w_ref[:, _OFF_QB:_OFF_KB]) * (DK_B ** -0.5)
    kb_ref[...] = _dot(xb, w_ref[:, _OFF_KB:_OFF_VB])
    vb_ref[...] = _dot(xb, w_ref[:, _OFF_VB:_OFF_GB])
    gb_ref[...] = _dot(xb, w_ref[:, _OFF_GB:_OFF_GLR])
    glr = _dot(xb, wglr_ref[...])
    z = _dot(glr, wgk2_ref[...], HI) + bgk2_ref[...]
    lg_ref[...] = (jnp.minimum(z, 0.0) - jnp.log1p(jnp.exp(-jnp.abs(z)))) * (1.0 / GATE_NORM)


def _inproj(x, w_main, w_glr, w_gk2p, b_gk2, cos_t, sin_t):
    n = x.shape[0]
    tm = TOKEN_TILE
    ntab = cos_t.shape[0] // tm
    row = lambda i: (i, 0)
    fixed = lambda i: (0, 0)
    tab = lambda i: (i % ntab, 0)
    widths = (W_A, W_A, W_A, H_B * DK_B, H_B * DK_B, W_B, W_B, H_B * DK_B)
    return pl.pallas_call(
        _inproj_kernel,
        grid=(n // tm,),
        in_specs=[
            pl.BlockSpec((tm, D_MODEL), row),
            pl.BlockSpec(w_main.shape, fixed),
            pl.BlockSpec(w_glr.shape, fixed),
            pl.BlockSpec(w_gk2p.shape, fixed),
            pl.BlockSpec(b_gk2.shape, fixed),
            pl.BlockSpec((tm, LANES), tab),
            pl.BlockSpec((tm, LANES), tab),
        ],
        out_specs=[pl.BlockSpec((tm, w), row) for w in widths],
        out_shape=[jax.ShapeDtypeStruct((n, w), F32) for w in widths],
        compiler_params=_cparams("parallel"),
        name="inproj",
    )(x, w_main, w_glr, w_gk2p, b_gk2, cos_t, sin_t)


def _pattn_kernel(lamq_ref, lamk_ref, g_ref, q_ref, k_ref, v_ref, o_ref, *, li):
    tq = q_ref.shape[0]
    qi = pl.program_id(2)
    q = q_ref[...]
    lane = lax.broadcasted_iota(I32, q.shape, 1)
    qq = jnp.concatenate([jnp.where(lane < HD_A, q, 0.0), jnp.where(lane >= HD_A, q, 0.0)],
                         axis=0).astype(BF16)

    def step(j, carry, masked):
        m, l, acc = carry
        start = pl.multiple_of(j * tq, tq)
        kb = k_ref[pl.ds(start, tq), :].astype(BF16)
        vb = v_ref[pl.ds(start, tq), :].astype(BF16)
        s = lax.dot_general(qq, kb, _NT, preferred_element_type=F32)
        if masked:
            row = lax.broadcasted_iota(I32, s.shape, 0) & (tq - 1)
            col = lax.broadcasted_iota(I32, s.shape, 1)
            s = jnp.where(col <= row, s, -jnp.inf)
        m_new = jnp.maximum(m, jnp.max(s, axis=1, keepdims=True))
        alpha = jnp.exp(m - m_new)
        p = jnp.exp(s - m_new)
        l = alpha * l + jnp.sum(p, axis=1, keepdims=True)
        acc = alpha * acc + _dot(p.astype(BF16), vb)
        return m_new, l, acc

    init = (jnp.full((2 * tq, 1), -jnp.inf, F32), jnp.zeros((2 * tq, 1), F32),
            jnp.zeros((2 * tq, DKV_A), F32))
    carry = lax.fori_loop(0, qi, lambda j, c: step(j, c, False), init)
    _, l, acc = step(qi, carry, True)
    o = acc / l
    out = o[:tq] - _lam(lamq_ref, lamk_ref, li) * o[tq:]
    o_ref[...] = _rms(out, g_ref[...]) * (1.0 - _lam_init(li))


def _prompt_attention(qa, ka, va, lamq, lamk, g, *, li, batch, seq):
    tq = ATTN_TILE
    assert tq & (tq - 1) == 0 and seq % tq == 0
    nq = seq // tq
    small = lambda b, h, i: (0, 0)
    return pl.pallas_call(
        functools.partial(_pattn_kernel, li=li),
        grid=(batch, H_A, nq),
        in_specs=[
            pl.BlockSpec(lamq.shape, small),
            pl.BlockSpec(lamk.shape, small),
            pl.BlockSpec(g.shape, small),
            pl.BlockSpec((tq, DKV_A), lambda b, h, i: (b * nq + i, h)),
            pl.BlockSpec((seq, DKV_A), lambda b, h, i: (b, h)),
            pl.BlockSpec((seq, DKV_A), lambda b, h, i: (b, h)),
        ],
        out_specs=pl.BlockSpec((tq, DKV_A), lambda b, h, i: (b * nq + i, h)),
        out_shape=jax.ShapeDtypeStruct(qa.shape, F32),
        compiler_params=_cparams("parallel", "parallel", "arbitrary"),
        name="prompt_attn",
    )(lamq, lamk, g, qa, ka, va)


def _sattn_kernel(pt_ref, lamq_ref, lamk_ref, g_ref, q_ref, kn_ref, vn_ref, *refs, li, npages, ts):
    k_pages = refs[:npages]
    v_pages = refs[npages:2 * npages]
    o_ref = refs[2 * npages]
    qt_scr, m_scr, l_scr, acc_scr, kn_scr, vn_scr = refs[2 * npages + 1:]
    step = pl.program_id(1)

    @pl.when(step == 0)
    def _():
        q = q_ref[...]
        q_rep = jnp.concatenate([q] * (H_A * 2), axis=0)
        row = lax.broadcasted_iota(I32, q_rep.shape, 0)
        lane = lax.broadcasted_iota(I32, q_rep.shape, 1)
        qt = jnp.where((row // ts) == (lane // HD_A), q_rep, 0.0).astype(BF16)
        qt_scr[...] = qt
        kn_scr[...] = jnp.zeros(kn_scr.shape, F32)
        vn_scr[...] = jnp.zeros(vn_scr.shape, F32)
        kn_scr[0:ts, :] = kn_ref[...]
        vn_scr[0:ts, :] = vn_ref[...]
        s = lax.dot_general(qt, kn_scr[...].astype(BF16), _NT, preferred_element_type=F32)
        r = lax.broadcasted_iota(I32, s.shape, 0) % ts
        c = lax.broadcasted_iota(I32, s.shape, 1)
        s = jnp.where(c <= r, s, -jnp.inf)
        m = jnp.max(s, axis=1, keepdims=True)
        p = jnp.exp(s - m)
        m_scr[...] = jnp.broadcast_to(m, m_scr.shape)
        l_scr[...] = jnp.broadcast_to(jnp.sum(p, axis=1, keepdims=True), l_scr.shape)
        acc_scr[...] = _dot(p.astype(BF16), vn_scr[...].astype(BF16))

    qt = qt_scr[...]
    s = jnp.concatenate(
        [lax.dot_general(qt, kp[...].astype(BF16), _NT, preferred_element_type=F32) for kp in k_pages],
        axis=1)
    m_prev = m_scr[:, 0:1]
    m_new = jnp.maximum(m_prev, jnp.max(s, axis=1, keepdims=True))
    alpha = jnp.exp(m_prev - m_new)
    p = jnp.exp(s - m_new)
    l_scr[...] = jnp.broadcast_to(alpha * l_scr[:, 0:1] + jnp.sum(p, axis=1, keepdims=True), l_scr.shape)
    m_scr[...] = jnp.broadcast_to(m_new, m_scr.shape)
    pv = None
    for i, vp in enumerate(v_pages):
        t = _dot(p[:, i * PAGE_SIZE:(i + 1) * PAGE_SIZE].astype(BF16), vp[...].astype(BF16))
        pv = t if pv is None else pv + t
    acc_scr[...] = alpha * acc_scr[...] + pv

    @pl.when(step == pl.num_programs(1) - 1)
    def _():
        o = acc_scr[...] / l_scr[:, 0:1]
        lam = _lam(lamq_ref, lamk_ref, li)
        for h in range(H_A):
            r0 = h * 2 * ts
            o1 = o[r0:r0 + ts, h * DKV_A:(h + 1) * DKV_A]
            o2 = o[r0 + ts:r0 + 2 * ts, h * DKV_A:(h + 1) * DKV_A]
            o_ref[:, h * DKV_A:(h + 1) * DKV_A] = _rms(o1 - lam * o2, g_ref[...]) * (1.0 - _lam_init(li))


def _sample_attention(qa, ka, va, cache_k, cache_v, page_table, lamq, lamk, g, *, li, batch, ts):
    npg = PAGES_PER_STEP
    n_pages = page_table.shape[1]
    assert n_pages % npg == 0 and ts == 8
    nrow = H_A * 2 * ts
    small = lambda b, s, pt: (0, 0)
    new = lambda b, s, pt: (b, 0)

    def page_spec(i):
        return pl.BlockSpec((None, None, PAGE_SIZE, W_A),
                            lambda b, s, pt, i=i: (pt[b, s * npg + i], li, 0, 0))

    grid_spec = pltpu.PrefetchScalarGridSpec(
        num_scalar_prefetch=1,
        grid=(batch, n_pages // npg),
        in_specs=[
            pl.BlockSpec(lamq.shape, small),
            pl.BlockSpec(lamk.shape, small),
            pl.BlockSpec(g.shape, small),
            pl.BlockSpec((ts, W_A), new),
            pl.BlockSpec((ts, W_A), new),
            pl.BlockSpec((ts, W_A), new),
        ] + [page_spec(i) for i in range(npg)] + [page_spec(i) for i in range(npg)],
        out_specs=pl.BlockSpec((ts, W_A), new),
        scratch_shapes=[
            pltpu.VMEM((nrow, W_A), BF16),
            pltpu.VMEM((nrow, LANES), F32),
            pltpu.VMEM((nrow, LANES), F32),
            pltpu.VMEM((nrow, W_A), F32),
            pltpu.VMEM((PAGE_SIZE, W_A), F32),
            pltpu.VMEM((PAGE_SIZE, W_A), F32),
        ],
    )
    ck = cache_k.reshape(cache_k.shape[0], DEPTH, PAGE_SIZE, W_A)
    cv = cache_v.reshape(cache_v.shape[0], DEPTH, PAGE_SIZE, W_A)
    return pl.pallas_call(
        functools.partial(_sattn_kernel, li=li, npages=npg, ts=ts),
        grid_spec=grid_spec,
        out_shape=jax.ShapeDtypeStruct(qa.shape, F32),
        compiler_params=_cparams("parallel", "arbitrary"),
        name="sample_attn",
    )(page_table, lamq, lamk, g, qa, ka, va, *([ck] * npg), *([cv] * npg))


def _gla_kernel(q_ref, k_ref, v_ref, lg_ref, s0_ref, g_ref, o_ref, sout_ref, s_scr, *, chunk):
    t = pl.program_id(1)

    @pl.when(t == 0)
    def _():
        s_scr[...] = s0_ref[...]

    row = lax.broadcasted_iota(I32, (chunk, chunk), 0)
    col = lax.broadcasted_iota(I32, (chunk, chunk), 1)
    causal = col <= row
    tri = causal.astype(F32)
    for c in range(q_ref.shape[0] // chunk):
        r0 = c * chunk
        for h in range(H_B):
            q = q_ref[r0:r0 + chunk, h * DK_B:(h + 1) * DK_B]
            k = k_ref[r0:r0 + chunk, h * DK_B:(h + 1) * DK_B]
            lg = lg_ref[r0:r0 + chunk, h * DK_B:(h + 1) * DK_B]
            v = v_ref[r0:r0 + chunk, h * DV_B:(h + 1) * DV_B]
            bc = _dot(tri, lg, HI)
            bl = bc[chunk - 1:chunk, :]
            qg = q * jnp.exp(bc)
            kg = k * jnp.exp(-bc)
            a = jnp.where(causal, lax.dot_general(qg, kg, _NT, preferred_element_type=F32, precision=HI), 0.0)
            s_prev = s_scr[h]
            o = _dot(qg, s_prev, HI) + _dot(a, v, HI)
            kd = k * jnp.exp(bl - bc)
            decay = jnp.transpose(jnp.broadcast_to(jnp.exp(bl), (DV_B, DK_B)))
            s_scr[h] = decay * s_prev + lax.dot_general(kd, v, _TN, preferred_element_type=F32, precision=HI)
            o_ref[r0:r0 + chunk, h * DV_B:(h + 1) * DV_B] = _rms(o, g_ref[...])

    @pl.when(t == pl.num_programs(1) - 1)
    def _():
        sout_ref[...] = s_scr[...]


def _gla(qb, kb, vb, lg, s0, g, *, batch, seq, chunk):
    tb = min(seq, GLA_TILE)
    assert seq % tb == 0 and tb % chunk == 0
    nt = seq // tb
    tok = lambda b, t: (b * nt + t, 0)
    st = lambda b, t: (b, 0, 0, 0)
    return pl.pallas_call(
        functools.partial(_gla_kernel, chunk=chunk),
        grid=(batch, nt),
        in_specs=[
            pl.BlockSpec((tb, H_B * DK_B), tok),
            pl.BlockSpec((tb, H_B * DK_B), tok),
            pl.BlockSpec((tb, W_B), tok),
            pl.BlockSpec((tb, H_B * DK_B), tok),
            pl.BlockSpec((None, H_B, DK_B, DV_B), st),
            pl.BlockSpec(g.shape, lambda b, t: (0, 0)),
        ],
        out_specs=[
            pl.BlockSpec((tb, W_B), tok),
            pl.BlockSpec((None, H_B, DK_B, DV_B), st),
        ],
        out_shape=[
            jax.ShapeDtypeStruct((batch * seq, W_B), F32),
            jax.ShapeDtypeStruct((batch, H_B, DK_B, DV_B), F32),
        ],
        scratch_shapes=[pltpu.VMEM((H_B, DK_B, DV_B), F32)],
        compiler_params=_cparams("parallel", "arbitrary"),
        name="gla",
    )(qb, kb, vb, lg, s0, g)


def _mix_kernel(x_ref, oa_ref, ob_ref, gb_ref, wo_ref, g1_ref, b1_ref, wr_ref, br_ref,
                x1_ref, gw_ref, idx_ref, cnt_ref, cnt_scr):
    i = pl.program_id(0)

    @pl.when(i == 0)
    def _():
        cnt_scr[...] = jnp.zeros(cnt_scr.shape, F32)

    gb = gb_ref[...]
    obg = ob_ref[...] * (gb * jax.nn.sigmoid(gb))
    mix = (_dot(oa_ref[...].astype(BF16), wo_ref[0:W_A, :])
           + _dot(obg.astype(BF16), wo_ref[W_A:D_MODEL, :]))
    x1 = _layernorm(ALPHA * x_ref[...] + mix, g1_ref[...], b1_ref[...])
    x1_ref[...] = x1

    logits = _dot(x1, wr_ref[...], HI) + br_ref[...]
    tm = logits.shape[0]
    lane = lax.broadcasted_iota(I32, logits.shape, 1)
    lane_f = lane.astype(F32)
    big = float(LANES)
    neg = -jnp.inf

    def first_argmax(v, vmax):
        return jnp.min(jnp.where(v == vmax, lane_f, big), axis=1, keepdims=True)

    lg1 = jnp.where((lane >= N_EXPERTS) & (lane < N_EXPERTS + N_GROUPS), logits, neg)
    m1 = jnp.max(lg1, axis=1, keepdims=True)
    pg = 1.0 / jnp.sum(jnp.exp(lg1 - m1), axis=1, keepdims=True)
    grp = first_argmax(lg1, m1) - float(N_EXPERTS)
    in_grp = (lane_f >= grp * EXP_PER_GROUP) & (lane_f < (grp + 1.0) * EXP_PER_GROUP)
    lg2 = jnp.where(in_grp, logits, neg)
    v1 = jnp.max(lg2, axis=1, keepdims=True)
    i1 = first_argmax(lg2, v1)
    lg2b = jnp.where(lane_f == i1, neg, lg2)
    v2 = jnp.max(lg2b, axis=1, keepdims=True)
    i2 = first_argmax(lg2b, v2)
    t = jnp.exp(v2 - v1)
    w1 = pg / (1.0 + t)
    w2 = pg * t / (1.0 + t)

    hit1 = lane_f == i1
    hit2 = lane_f == i2
    sel = jnp.where(hit1 | hit2, 1.0, 0.0)
    r = lax.broadcasted_iota(I32, (tm, tm), 0)
    c = lax.broadcasted_iota(I32, (tm, tm), 1)
    earlier = jnp.where(c < r, 1.0, 0.0).astype(BF16)
    rank = _dot(earlier, sel.astype(BF16)) + cnt_scr[...]
    pos1 = jnp.sum(jnp.where(hit1, rank, 0.0), axis=1, keepdims=True)
    pos2 = jnp.sum(jnp.where(hit2, rank, 0.0), axis=1, keepdims=True)
    cnt = cnt_scr[...] + jnp.sum(sel, axis=0, keepdims=True)
    cnt_scr[...] = cnt
    cnt_ref[...] = cnt.astype(I32)

    gw_ref[...] = jnp.where(lane == 0, w1, jnp.where(lane == 1, w2, 0.0))
    idx_f = jnp.where(lane == 0, i1, jnp.where(lane == 1, i2, jnp.where(lane == 2, pos1,
                      jnp.where(lane == 3, pos2, 0.0))))
    idx_ref[...] = idx_f.astype(I32)


def _mix(x, oa, ob, gb, wo, g1, b1, wr, br):
    n = x.shape[0]
    tm = TOKEN_TILE
    row = lambda i: (i, 0)
    fixed = lambda i: (0, 0)
    return pl.pallas_call(
        _mix_kernel,
        grid=(n // tm,),
        in_specs=[
            pl.BlockSpec((tm, D_MODEL), row),
            pl.BlockSpec((tm, W_A), row),
            pl.BlockSpec((tm, W_B), row),
            pl.BlockSpec((tm, W_B), row),
            pl.BlockSpec(wo.shape, fixed),
            pl.BlockSpec(g1.shape, fixed),
            pl.BlockSpec(b1.shape, fixed),
            pl.BlockSpec(wr.shape, fixed),
            pl.BlockSpec(br.shape, fixed),
        ],
        out_specs=[
            pl.BlockSpec((tm, D_MODEL), row),
            pl.BlockSpec((tm, LANES), row),
            pl.BlockSpec((tm, LANES), row),
            pl.BlockSpec((1, LANES), fixed),
        ],
        out_shape=[
            jax.ShapeDtypeStruct((n, D_MODEL), F32),
            jax.ShapeDtypeStruct((n, LANES), F32),
            jax.ShapeDtypeStruct((n, LANES), I32),
            jax.ShapeDtypeStruct((1, LANES), I32),
        ],
        scratch_shapes=[pltpu.VMEM((1, LANES), F32)],
        compiler_params=_cparams("arbitrary"),
        name="mix_route",
    )(x, oa, ob, gb, wo, g1, b1, wr, br)


def _row_copy(src, dst, sem):
    return pltpu.make_async_copy(src, dst, sem)


def _dispatch_kernel(pstart_ref, idx_ref, x_ref, buf_in_ref, buf_ref, sem):
    del buf_in_ref
    tm = x_ref.shape[0]

    def issue(t, carry):
        for k in range(2):
            dst = pstart_ref[idx_ref[0, 0, 4 * t + k]] + idx_ref[0, 0, 4 * t + 2 + k]
            _row_copy(x_ref.at[pl.ds(t, 1), :], buf_ref.at[pl.ds(dst, 1), :], sem).start()
        return carry

    lax.fori_loop(0, tm, issue, 0)

    def drain(t, carry):
        for k in range(2):
            _row_copy(x_ref.at[pl.ds(0, 1), :], buf_ref.at[pl.ds(0, 1), :], sem).wait()
        return carry

    lax.fori_loop(0, tm, drain, 0)


def _dispatch(pstart, idx, x1, n_rows):
    n = x1.shape[0]
    tm = TOKEN_TILE
    grid_spec = pltpu.PrefetchScalarGridSpec(
        num_scalar_prefetch=1,
        grid=(n // tm,),
        in_specs=[
            pl.BlockSpec((1, 1, 4 * tm), lambda i, ps: (i, 0, 0), memory_space=pltpu.SMEM),
            pl.BlockSpec((tm, D_MODEL), lambda i, ps: (i, 0)),
            pl.BlockSpec(memory_space=pl.ANY),
        ],
        out_specs=pl.BlockSpec(memory_space=pl.ANY),
        scratch_shapes=[pltpu.SemaphoreType.DMA(())],
    )
    return pl.pallas_call(
        _dispatch_kernel,
        grid_spec=grid_spec,
        out_shape=jax.ShapeDtypeStruct((n_rows, D_MODEL), F32),
        input_output_aliases={3: 0},
        compiler_params=_cparams("arbitrary"),
        name="dispatch",
    )(pstart, idx, x1, jnp.zeros((n_rows, D_MODEL), F32))


def _expert_kernel(be_ref, nu_ref, x_ref, wg_ref, wu_ref, wd_ref, y_ref, wgb, wub, wdb):
    j = pl.program_id(0)

    @pl.when(j < nu_ref[0])
    def _():
        prev = be_ref[jnp.maximum(j - 1, 0)]

        @pl.when((j == 0) | (be_ref[j] != prev))
        def _():
            wgb[...] = wg_ref[...].astype(BF16)
            wub[...] = wu_ref[...].astype(BF16)
            wdb[...] = wd_ref[...].astype(BF16)

        x = x_ref[...].astype(BF16)
        hg = _dot(x, wgb[...])
        hu = _dot(x, wub[...])
        h = hg * jax.nn.sigmoid(hg) * hu
        y_ref[...] = _dot(h.astype(BF16), wdb[...])

    @pl.when(j >= nu_ref[0])
    def _():
        y_ref[...] = jnp.zeros(y_ref.shape, F32)


def _experts(block_e, n_used, xbuf, w_gate, w_up, w_down, *, li):
    n_rows = xbuf.shape[0]
    blk = EXPERT_BLOCK
    nb = n_rows // blk

    def rows(j, be, nu):
        return (jnp.minimum(j, nu[0] - 1), 0)

    def wsel(j, be, nu):
        return (li, be[jnp.minimum(j, nu[0] - 1)], 0, 0)

    grid_spec = pltpu.PrefetchScalarGridSpec(
        num_scalar_prefetch=2,
        grid=(nb,),
        in_specs=[
            pl.BlockSpec((blk, D_MODEL), rows),
            pl.BlockSpec((None, None, D_MODEL, D_EXPERT), wsel),
            pl.BlockSpec((None, None, D_MODEL, D_EXPERT), wsel),
            pl.BlockSpec((None, None, D_EXPERT, D_MODEL), wsel),
        ],
        out_specs=pl.BlockSpec((blk, D_MODEL), lambda j, be, nu: (j, 0)),
        scratch_shapes=[
            pltpu.VMEM((D_MODEL, D_EXPERT), BF16),
            pltpu.VMEM((D_MODEL, D_EXPERT), BF16),
            pltpu.VMEM((D_EXPERT, D_MODEL), BF16),
        ],
    )
    return pl.pallas_call(
        _expert_kernel,
        grid_spec=grid_spec,
        out_shape=jax.ShapeDtypeStruct((n_rows, D_MODEL), F32),
        compiler_params=_cparams("arbitrary"),
        name="experts",
    )(block_e, n_used, xbuf, w_gate, w_up, w_down)


def _combine_kernel(pstart_ref, idx_ref, gw_ref, x1_ref, p_ref, ybuf_ref, wpp_ref, wpg_ref,
                    g2_ref, b2_ref, gp_ref, o_ref, y_scr, sem):
    tm = x1_ref.shape[0]

    def issue(t, carry):
        for k in range(2):
            src = pstart_ref[idx_ref[0, 0, 4 * t + k]] + idx_ref[0, 0, 4 * t + 2 + k]
            _row_copy(ybuf_ref.at[pl.ds(src, 1), :], y_scr.at[k, pl.ds(t, 1), :], sem).start()
        return carry

    lax.fori_loop(0, tm, issue, 0)

    def drain(t, carry):
        for k in range(2):
            _row_copy(ybuf_ref.at[pl.ds(0, 1), :], y_scr.at[k, pl.ds(0, 1), :], sem).wait()
        return carry

    lax.fori_loop(0, tm, drain, 0)

    gw = gw_ref[...]
    y = gw[:, 0:1] * y_scr[0] + gw[:, 1:2] * y_scr[1]
    x2 = _layernorm(ALPHA * x1_ref[...] + y, g2_ref[...], b2_ref[...])
    e = _rms(_dot(p_ref[...].astype(BF16), wpp_ref[...]), gp_ref[...])
    o_ref[...] = x2 + jax.nn.sigmoid(_dot(x2.astype(BF16), wpg_ref[...])) * e


def _combine(pstart, idx, gw, x1, p_l, ybuf, wpp, wpg, g2, b2, gp):
    n = x1.shape[0]
    tm = TOKEN_TILE
    row = lambda i, ps: (i, 0)
    fixed = lambda i, ps: (0, 0)
    grid_spec = pltpu.PrefetchScalarGridSpec(
        num_scalar_prefetch=1,
        grid=(n // tm,),
        in_specs=[
            pl.BlockSpec((1, 1, 4 * tm), lambda i, ps: (i, 0, 0), memory_space=pltpu.SMEM),
            pl.BlockSpec((tm, LANES), row),
            pl.BlockSpec((tm, D_MODEL), row),
            pl.BlockSpec((tm, D_PLE), row),
            pl.BlockSpec(memory_space=pl.ANY),
            pl.BlockSpec(wpp.shape, fixed),
            pl.BlockSpec(wpg.shape, fixed),
            pl.BlockSpec(g2.shape, fixed),
            pl.BlockSpec(b2.shape, fixed),
            pl.BlockSpec(gp.shape, fixed),
        ],
        out_specs=pl.BlockSpec((tm, D_MODEL), row),
        scratch_shapes=[pltpu.VMEM((2, tm, D_MODEL), F32), pltpu.SemaphoreType.DMA(())],
    )
    return pl.pallas_call(
        _combine_kernel,
        grid_spec=grid_spec,
        out_shape=jax.ShapeDtypeStruct((n, D_MODEL), F32),
        compiler_params=_cparams("arbitrary"),
        name="combine",
    )(pstart, idx, gw, x1, p_l, ybuf, wpp, wpg, g2, b2, gp)


def _rope_tables(pos):
    half = HD_A // 2
    inv = ROPE_THETA ** (-jnp.arange(half, dtype=F32) / half)
    ang = pos.astype(F32)[:, None] * inv[None, :]
    cos, sin = jnp.cos(ang), jnp.sin(ang)
    reps = LANES // HD_A
    return (jnp.tile(jnp.concatenate([cos, cos], axis=1), (1, reps)),
            jnp.tile(jnp.concatenate([-sin, sin], axis=1), (1, reps)))


def _moe_plan(counts, n_tokens):
    blk = EXPERT_BLOCK
    nb = (2 * n_tokens) // blk + N_EXPERTS
    padded = (counts + blk - 1) // blk * blk
    pends = jnp.cumsum(padded)
    pstart = (pends - padded).astype(I32)
    block_e = jnp.minimum(jnp.searchsorted(pends, jnp.arange(nb, dtype=I32) * blk, side="right"),
                          N_EXPERTS - 1).astype(I32)
    n_used = (pends[-1:] // blk).astype(I32)
    return pstart, block_e, n_used, nb * blk


def _layer(x, p_l, li, lw, rope, attend, s0, *, batch, seq, chunk):
    cos_t, sin_t = rope
    qa, ka, va, qb, kb, vb, gb, lg = _inproj(x, lw["w_main"], lw["w_glr"], lw["w_gk2"], lw["b_gk2"],
                                             cos_t, sin_t)
    oa = attend(qa, ka, va)
    ob, s_out = _gla(qb, kb, vb, lg, s0, lw["gla_g"], batch=batch, seq=seq, chunk=chunk)
    x1, gw, idx, cnt = _mix(x, oa, ob, gb, lw["w_o"], lw["ln1_g"], lw["ln1_b"], lw["w_r"], lw["b_r"])
    n = x.shape[0]
    pstart, block_e, n_used, n_rows = _moe_plan(cnt[0, :N_EXPERTS], n)
    idx4 = idx[:, :4].reshape(n // TOKEN_TILE, 1, 4 * TOKEN_TILE)
    xbuf = _dispatch(pstart, idx4, x1, n_rows)
    ybuf = _experts(block_e, n_used, xbuf, lw["w_gate"], lw["w_up"], lw["w_down"], li=li)
    out = _combine(pstart, idx4, gw, x1, p_l, ybuf, lw["w_pp"], lw["w_pg"],
                   lw["ln2_g"], lw["ln2_b"], lw["ple_g"])
    return out, ka, va, s_out


def kernel(x_prompt, x_sample, cache_k, cache_v, state_gla, page_table, p_prompt, p_sample, w_in, w_gk2, b_gk2, lam_q1, lam_k1, lam_q2, lam_k2, diff_norm_g, gla_norm_g, w_o, ln1_g, ln1_b, w_r1, b_r1, w_r2, b_r2, w_gate, w_up, w_down, ln2_g, ln2_b, w_ple_gate, w_ple_proj, ple_norm_g):
    bp, tp, _ = x_prompt.shape
    bs, ts, _ = x_sample.shape
    past = page_table.shape[1] * PAGE_SIZE
    rope_p = _rope_tables(jnp.arange(tp))
    rope_s = tuple(jnp.tile(t, (TOKEN_TILE // ts, 1)) for t in _rope_tables(past + jnp.arange(ts)))

    row2 = lambda a: a.reshape(1, -1)
    yp = x_prompt.reshape(bp * tp, D_MODEL)
    ys = x_sample.reshape(bs * ts, D_MODEL)
    s0_p = jnp.zeros((bp, H_B, DK_B, DV_B), F32)
    outs = {k: [] for k in ("kp", "vp", "sp", "ks", "vs", "ss")}
    for li in range(DEPTH):
        lw = {
            "w_main": w_in[li, :, :_OFF_GLR].astype(BF16),
            "w_glr": jnp.pad(w_in[li, :, _OFF_GLR:], ((0, 0), (0, LANES - GATE_RANK))).astype(BF16),
            "w_gk2": jnp.pad(w_gk2[li], ((0, LANES - GATE_RANK), (0, 0))),
            "b_gk2": row2(b_gk2[li]),
            "gla_g": row2(gla_norm_g[li]),
            "w_o": w_o[li].astype(BF16),
            "ln1_g": row2(ln1_g[li]), "ln1_b": row2(ln1_b[li]),
            "w_r": jnp.pad(jnp.concatenate([w_r2[li], w_r1[li]], axis=1),
                           ((0, 0), (0, LANES - N_EXPERTS - N_GROUPS))),
            "b_r": row2(jnp.pad(jnp.concatenate([b_r2[li], b_r1[li]]), (0, LANES - N_EXPERTS - N_GROUPS))),
            "w_gate": w_gate, "w_up": w_up, "w_down": w_down,
            "w_pp": w_ple_proj[li].astype(BF16),
            "w_pg": w_ple_gate[li].astype(BF16),
            "ln2_g": row2(ln2_g[li]), "ln2_b": row2(ln2_b[li]),
            "ple_g": row2(ple_norm_g[li]),
        }
        lamq = jnp.stack([lam_q1[li], lam_q2[li]])
        lamk = jnp.stack([lam_k1[li], lam_k2[li]])
        dg = row2(diff_norm_g[li])

        attend_p = functools.partial(_prompt_attention, lamq=lamq, lamk=lamk, g=dg, li=li, batch=bp, seq=tp)
        yp, ka, va, s = _layer(yp, p_prompt[li].reshape(bp * tp, D_PLE), li, lw, rope_p, attend_p, s0_p,
                               batch=bp, seq=tp, chunk=GLA_CHUNK)
        outs["kp"].append(ka.reshape(bp, tp, H_A, DKV_A))
        outs["vp"].append(va.reshape(bp, tp, H_A, DKV_A))
        outs["sp"].append(s)

        attend_s = functools.partial(_sample_attention, cache_k=cache_k, cache_v=cache_v,
                                     page_table=page_table, lamq=lamq, lamk=lamk, g=dg,
                                     li=li, batch=bs, ts=ts)
        ys, ka, va, s = _layer(ys, p_sample[li].reshape(bs * ts, D_PLE), li, lw, rope_s, attend_s,
                               state_gla[:, li], batch=bs, seq=ts, chunk=ts)
        outs["ks"].append(ka.reshape(bs, ts, H_A, DKV_A))
        outs["vs"].append(va.reshape(bs, ts, H_A, DKV_A))
        outs["ss"].append(s)

    stack = lambda k: jnp.stack(outs[k], axis=1)
    return (yp.reshape(bp, tp, D_MODEL), ys.reshape(bs, ts, D_MODEL),
            stack("kp"), stack("vp"), stack("sp"), stack("ks"), stack("vs"), stack("ss"))
```

```python
import functools
import math

import jax
import jax.numpy as jnp
from jax import lax
from jax.experimental import pallas as pl
from jax.experimental.pallas import tpu as pltpu

F32 = jnp.float32
BF16 = jnp.bfloat16
I32 = jnp.int32
HI = lax.Precision.HIGHEST

D_MODEL = 1024
DEPTH = 2
PAGE_SIZE = 128
D_PLE = 256
HD_A = 64
DKV_A = 2 * HD_A
W_A = D_MODEL // 2
H_A = W_A // DKV_A
W_B = D_MODEL - W_A
H_B = 4
DV_B = W_B // H_B
DK_B = DV_B // 2
GATE_RANK = 16
GATE_NORM = 16.0
GLA_CHUNK = 64
N_GROUPS = 4
EXP_PER_GROUP = 8
N_EXPERTS = N_GROUPS * EXP_PER_GROUP
D_EXPERT = D_MODEL // 2
ROPE_THETA = 10000.0
ALPHA = (2 * DEPTH) ** 0.25
EPS = 1e-5
LOG2E = math.log2(math.e)

LANES = 128
VMEM_LIMIT = 48 * 1024 * 1024

TOKEN_TILE = 256
ATTN_TILE = 256
PAGES_PER_STEP = 8
GLA_TILE = 256
EXPERT_BLOCK = 256

_OFF_QA, _OFF_KA, _OFF_VA = 0, W_A, 2 * W_A
_OFF_QB = 3 * W_A
_OFF_KB = _OFF_QB + H_B * DK_B
_OFF_VB = _OFF_KB + H_B * DK_B
_OFF_GB = _OFF_VB + W_B
_OFF_GLR = _OFF_GB + W_B

_NT = (((1,), (1,)), ((), ()))
_TN = (((0,), (0,)), ((), ()))


def _cparams(*sem):
    return pltpu.CompilerParams(dimension_semantics=sem, vmem_limit_bytes=VMEM_LIMIT)


def _dot(a, b, precision=None):
    return jnp.dot(a, b, preferred_element_type=F32, precision=precision)


def _lam_init(li):
    return 0.8 - 0.6 * math.exp(-0.3 * li)


def _lam(lamq_ref, lamk_ref, li):
    s = jnp.sum(lamq_ref[...] * lamk_ref[...], axis=1, keepdims=True)
    e = jnp.exp(s)
    return e[0:1, :] - e[1:2, :] + _lam_init(li)


def _rms(x, g):
    return x * lax.rsqrt(jnp.mean(x * x, axis=-1, keepdims=True) + EPS) * g


def _layernorm(x, g, b):
    mu = jnp.mean(x, axis=-1, keepdims=True)
    xc = x - mu
    var = jnp.mean(xc * xc, axis=-1, keepdims=True)
    return xc * lax.rsqrt(var + EPS) * g + b


def _inproj_kernel(x_ref, w_ref, wvt_ref, wglr_ref, wgk2_ref, bgk2_ref, cos_ref, sin_ref,
                   qa_ref, ka_ref, va_ref, qb_ref, kb_ref, vb_ref, gb_ref, lg_ref, vt_ref):
    xb = x_ref[...].astype(BF16)
    vt_ref[...] = lax.dot_general(wvt_ref[...], xb, _NT, preferred_element_type=F32).astype(BF16)
    cos = cos_ref[...]
    sin = sin_ref[...]
    lane = lax.broadcasted_iota(I32, cos.shape, 1)
    first_half = (lane & (HD_A // 2)) == 0

    def rope(h):
        partner = jnp.where(first_half, pltpu.roll(h, LANES - HD_A // 2, 1), pltpu.roll(h, HD_A // 2, 1))
        return h * cos + partner * sin

    for c in range(W_A // LANES):
        lo = c * LANES
        hq = _dot(xb, w_ref[:, _OFF_QA + lo:_OFF_QA + lo + LANES])
        qa_ref[:, lo:lo + LANES] = rope(hq) * (HD_A ** -0.5 * LOG2E)
        hk = _dot(xb, w_ref[:, _OFF_KA + lo:_OFF_KA + lo + LANES])
        ka_ref[:, lo:lo + LANES] = rope(hk)
    va_ref[...] = _dot(xb, w_ref[:, _OFF_VA:_OFF_VA + W_A])
    qb_ref[...] = _dot(xb, w_ref[:, _OFF_QB:_OFF_KB]) * (DK_B ** -0.5)
    kb_ref[...] = _dot(xb, w_ref[:, _OFF_KB:_OFF_VB])
    vb_ref[...] = _dot(xb, w_ref[:, _OFF_VB:_OFF_GB])
    gb_ref[...] = _dot(xb, w_ref[:, _OFF_GB:_OFF_GLR])
    glr = _dot(xb, wglr_ref[...])
    z = _dot(glr, wgk2_ref[...], HI) + bgk2_ref[...]
    lg_ref[...] = (jnp.minimum(z, 0.0) - jnp.log1p(jnp.exp(-jnp.abs(z)))) * (1.0 / GATE_NORM)


def _inproj(x, w_main, w_vt, w_glr, w_gk2p, b_gk2, cos_t, sin_t):
    n = x.shape[0]
    tm = TOKEN_TILE
    ntab = cos_t.shape[0] // tm
    row = lambda i: (i, 0)
    fixed = lambda i: (0, 0)
    tab = lambda i: (i % ntab, 0)
    widths = (W_A, W_A, W_A, H_B * DK_B, H_B * DK_B, W_B, W_B, H_B * DK_B)
    return pl.pallas_call(
        _inproj_kernel,
        grid=(n // tm,),
        in_specs=[
            pl.BlockSpec((tm, D_MODEL), row),
            pl.BlockSpec(w_main.shape, fixed),
            pl.BlockSpec(w_vt.shape, fixed),
            pl.BlockSpec(w_glr.shape, fixed),
            pl.BlockSpec(w_gk2p.shape, fixed),
            pl.BlockSpec(b_gk2.shape, fixed),
            pl.BlockSpec((tm, LANES), tab),
            pl.BlockSpec((tm, LANES), tab),
        ],
        out_specs=[pl.BlockSpec((tm, w), row) for w in widths]
        + [pl.BlockSpec((None, W_A, tm), lambda i: (i, 0, 0))],
        out_shape=[jax.ShapeDtypeStruct((n, w), F32) for w in widths]
        + [jax.ShapeDtypeStruct((n // tm, W_A, tm), BF16)],
        compiler_params=_cparams("parallel"),
        name="inproj",
    )(x, w_main, w_vt, w_glr, w_gk2p, b_gk2, cos_t, sin_t)


def _pattn_kernel(lamq_ref, lamk_ref, g_ref, q_ref, k_ref, vt_ref, o_ref, acc_scr, sa_scr, sb_scr, *, li):
    tq = q_ref.shape[0]
    win = 2 * tq
    last_win = k_ref.shape[0] // win - 1
    qi = pl.program_id(2)
    n_pairs = lax.shift_right_logical(qi, 2)
    q = q_ref[...]
    lane = lax.broadcasted_iota(I32, q.shape, 1)
    qc = (jnp.where(lane < HD_A, q, 0.0).astype(BF16), jnp.where(lane >= HD_A, q, 0.0).astype(BF16))
    acc_scr[...] = jnp.zeros(acc_scr.shape, F32)

    def scores(w, s_scr):
        w = jnp.minimum(w, last_win)
        start = pl.multiple_of(w * win, win)
        kb = k_ref[pl.ds(start, win), :].astype(BF16)
        for c in range(2):
            s_scr[c] = lax.dot_general(kb, qc[c], _NT, preferred_element_type=F32)

    def consume(w, s_scr, stats, masked):
        vt0 = vt_ref[2 * w]
        vt1 = vt_ref[2 * w + 1]
        out = []
        for c in range(2):
            m_prev, l_prev = stats[c]
            st = s_scr[c]
            if masked:
                key = lax.broadcasted_iota(I32, st.shape, 0)
                qry = lax.broadcasted_iota(I32, (1, tq), 1) + (qi * tq - w * win)
                st = jnp.where(key <= qry, st, -jnp.inf)
            m_new = jnp.maximum(m_prev, jnp.max(st, axis=0, keepdims=True))
            alpha = jnp.exp2(m_prev - m_new)
            p = jnp.exp2(st - m_new)
            l_new = alpha * l_prev + jnp.sum(p, axis=0, keepdims=True)
            p = p.astype(BF16)
            cols = slice(c * tq, (c + 1) * tq)
            acc_scr[:, cols] = alpha * acc_scr[:, cols] + (_dot(vt0, p[0:tq]) + _dot(vt1, p[tq:win]))
            out.append((m_new, l_new))
        return tuple(out)

    def pair(i, stats, masked):
        scores(2 * i + 1, sb_scr)
        stats = consume(2 * i, sa_scr, stats, masked)
        if not masked:
            scores(2 * i + 2, sa_scr)
        return consume(2 * i + 1, sb_scr, stats, masked)

    scores(0, sa_scr)
    stat = (jnp.full((1, tq), -jnp.inf, F32), jnp.zeros((1, tq), F32))
    stats = lax.fori_loop(0, n_pairs, lambda i, s: pair(i, s, False), (stat, stat))
    (_, l1), (_, l2) = pair(n_pairs, stats, True)
    out_t = acc_scr[:, 0:tq] / l1 - _lam(lamq_ref, lamk_ref, li) * (acc_scr[:, tq:2 * tq] / l2)
    o_ref[...] = _rms(jnp.transpose(out_t), g_ref[...]) * (1.0 - _lam_init(li))


def _prompt_attention(qa, ka, vt, lamq, lamk, g, *, li, batch, seq):
    tq = ATTN_TILE
    assert tq == TOKEN_TILE and seq % (4 * tq) == 0
    nq = seq // tq
    small = lambda b, h, i: (0, 0)
    return pl.pallas_call(
        functools.partial(_pattn_kernel, li=li),
        grid=(batch, H_A, nq),
        in_specs=[
            pl.BlockSpec(lamq.shape, small),
            pl.BlockSpec(lamk.shape, small),
            pl.BlockSpec(g.shape, small),
            pl.BlockSpec((tq, DKV_A), lambda b, h, i: (b * nq + i, h)),
            pl.BlockSpec((seq, DKV_A), lambda b, h, i: (b, h)),
            pl.BlockSpec((nq, DKV_A, tq), lambda b, h, i: (b, h, 0)),
        ],
        out_specs=pl.BlockSpec((tq, DKV_A), lambda b, h, i: (b * nq + i, h)),
        out_shape=jax.ShapeDtypeStruct(qa.shape, F32),
        scratch_shapes=[pltpu.VMEM((DKV_A, 2 * tq), F32),
                        pltpu.VMEM((2, 2 * tq, tq), F32),
                        pltpu.VMEM((2, 2 * tq, tq), F32)],
        compiler_params=_cparams("parallel", "parallel", "arbitrary"),
        name="prompt_attn",
    )(lamq, lamk, g, qa, ka, vt)


def _sattn_kernel(pt_ref, lamq_ref, lamk_ref, g_ref, q_ref, kn_ref, vn_ref, *refs, li, npages, ts):
    k_pages = refs[:npages]
    v_pages = refs[npages:2 * npages]
    o_ref = refs[2 * npages]
    qt_scr, m_scr, l_scr, acc_scr, kn_scr, vn_scr = refs[2 * npages + 1:]
    step = pl.program_id(1)
    nrow = H_A * 2 * ts
    rows_per_page = PAGE_SIZE * H_A

    def head_match(shape):
        row = lax.broadcasted_iota(I32, shape, 0)
        col = lax.broadcasted_iota(I32, shape, 1)
        return (col & (H_A - 1)) == (row // (2 * ts)), row, col

    def update(state, scores, values):
        m_prev, l_prev, acc = state
        m_new = m_prev
        for s in scores:
            m_new = jnp.maximum(m_new, jnp.max(s, axis=1, keepdims=True))
        alpha = jnp.exp2(m_prev - m_new)
        l_new = alpha * l_prev
        acc = alpha * acc
        for s, v_bf in zip(scores, values):
            p = jnp.exp2(s - m_new)
            l_new = l_new + jnp.sum(p, axis=1, keepdims=True)
            acc = acc + _dot(p.astype(BF16), v_bf)
        return m_new, l_new, acc

    @pl.when(step == 0)
    def _():
        q = q_ref[...]
        lane = lax.broadcasted_iota(I32, (ts, DKV_A), 1)
        blocks = []
        for h in range(H_A):
            qh = q[:, h * DKV_A:(h + 1) * DKV_A]
            blocks += [jnp.where(lane < HD_A, qh, 0.0), jnp.where(lane >= HD_A, qh, 0.0)]
        qt = jnp.concatenate(blocks, axis=0).astype(BF16)
        qt_scr[...] = qt
        kn_scr[...] = jnp.zeros(kn_scr.shape, F32)
        vn_scr[...] = jnp.zeros(vn_scr.shape, F32)
        kn_scr[0:ts * H_A, :] = kn_ref[...]
        vn_scr[0:ts * H_A, :] = vn_ref[...]
        s = lax.dot_general(qt, kn_scr[...].astype(BF16), _NT, preferred_element_type=F32)
        match, row, col = head_match(s.shape)
        ok = match & ((col // H_A) <= (row & (ts - 1))) & (col < ts * H_A)
        s = jnp.where(ok, s, -jnp.inf)
        init = (jnp.full((nrow, 1), -jnp.inf, F32), jnp.zeros((nrow, 1), F32), jnp.zeros((nrow, DKV_A), F32))
        m, l, acc = update(init, [s], [vn_scr[...].astype(BF16)])
        m_scr[...] = jnp.broadcast_to(m, m_scr.shape)
        l_scr[...] = jnp.broadcast_to(l, l_scr.shape)
        acc_scr[...] = acc

    qt = qt_scr[...]
    match, _, _ = head_match((nrow, rows_per_page))
    scores = [jnp.where(match, lax.dot_general(qt, kp[...].astype(BF16), _NT, preferred_element_type=F32),
                        -jnp.inf) for kp in k_pages]
    state = (m_scr[:, 0:1], l_scr[:, 0:1], acc_scr[...])
    half = npages // 2
    for grp in (slice(0, half), slice(half, npages)):
        state = update(state, scores[grp], [vp[...].astype(BF16) for vp in v_pages[grp]])
    m, l, acc = state
    m_scr[...] = jnp.broadcast_to(m, m_scr.shape)
    l_scr[...] = jnp.broadcast_to(l, l_scr.shape)
    acc_scr[...] = acc

    @pl.when(step == pl.num_programs(1) - 1)
    def _():
        o = acc_scr[...] / l_scr[:, 0:1]
        lam = _lam(lamq_ref, lamk_ref, li)
        for h in range(H_A):
            r0 = h * 2 * ts
            out = o[r0:r0 + ts, :] - lam * o[r0 + ts:r0 + 2 * ts, :]
            o_ref[:, h * DKV_A:(h + 1) * DKV_A] = _rms(out, g_ref[...]) * (1.0 - _lam_init(li))


def _sample_attention(qa, ka, va, cache_k, cache_v, page_table, lamq, lamk, g, *, li, batch, ts):
    npg = PAGES_PER_STEP
    n_pages = page_table.shape[1]
    assert n_pages % npg == 0 and ts == 8 and ts * H_A <= PAGE_SIZE
    nrow = H_A * 2 * ts
    rows_per_page = PAGE_SIZE * H_A
    small = lambda b, s, pt: (0, 0)
    new = lambda b, s, pt: (b, 0)

    def page_spec(i):
        return pl.BlockSpec((None, None, rows_per_page, DKV_A),
                            lambda b, s, pt, i=i: (pt[b, s * npg + i], li, 0, 0))

    grid_spec = pltpu.PrefetchScalarGridSpec(
        num_scalar_prefetch=1,
        grid=(batch, n_pages // npg),
        in_specs=[
            pl.BlockSpec(lamq.shape, small),
            pl.BlockSpec(lamk.shape, small),
            pl.BlockSpec(g.shape, small),
            pl.BlockSpec((ts, W_A), new),
            pl.BlockSpec((ts * H_A, DKV_A), new),
            pl.BlockSpec((ts * H_A, DKV_A), new),
        ] + [page_spec(i) for i in range(npg)] + [page_spec(i) for i in range(npg)],
        out_specs=pl.BlockSpec((ts, W_A), new),
        scratch_shapes=[
            pltpu.VMEM((nrow, DKV_A), BF16),
            pltpu.VMEM((nrow, LANES), F32),
            pltpu.VMEM((nrow, LANES), F32),
            pltpu.VMEM((nrow, DKV_A), F32),
            pltpu.VMEM((PAGE_SIZE, DKV_A), F32),
            pltpu.VMEM((PAGE_SIZE, DKV_A), F32),
        ],
    )
    ck = cache_k.reshape(cache_k.shape[0], DEPTH, rows_per_page, DKV_A)
    cv = cache_v.reshape(cache_v.shape[0], DEPTH, rows_per_page, DKV_A)
    kn = ka.reshape(batch * ts * H_A, DKV_A)
    vn = va.reshape(batch * ts * H_A, DKV_A)
    return pl.pallas_call(
        functools.partial(_sattn_kernel, li=li, npages=npg, ts=ts),
        grid_spec=grid_spec,
        out_shape=jax.ShapeDtypeStruct(qa.shape, F32),
        compiler_params=_cparams("parallel", "arbitrary"),
        name="sample_attn",
    )(page_table, lamq, lamk, g, qa, kn, vn, *([ck] * npg), *([cv] * npg))


def _gla_kernel(q_ref, k_ref, v_ref, lg_ref, s0_ref, g_ref, o_ref, sout_ref, s_scr, *, chunk):
    t = pl.program_id(1)

    @pl.when(t == 0)
    def _():
        s_scr[...] = s0_ref[...]

    row = lax.broadcasted_iota(I32, (chunk, chunk), 0)
    col = lax.broadcasted_iota(I32, (chunk, chunk), 1)
    causal = col <= row
    tri = causal.astype(F32)
    for c in range(q_ref.shape[0] // chunk):
        rows = slice(c * chunk, (c + 1) * chunk)
        bc = _dot(tri, lg_ref[rows, :], HI)
        bl = bc[chunk - 1:chunk, :]
        k_all = k_ref[rows, :]
        qg_all = (q_ref[rows, :] * jnp.exp(bc)).astype(BF16)
        kg_all = (k_all * jnp.exp(-bc)).astype(BF16)
        kd_all = (k_all * jnp.exp(bl - bc)).astype(BF16)
        decay_all = jnp.transpose(jnp.broadcast_to(jnp.exp(bl), (DV_B, H_B * DK_B)))
        for h in range(H_B):
            dk = slice(h * DK_B, (h + 1) * DK_B)
            dv = slice(h * DV_B, (h + 1) * DV_B)
            qg = qg_all[:, dk]
            v = v_ref[rows, dv].astype(BF16)
            a = jnp.where(causal, lax.dot_general(qg, kg_all[:, dk], _NT, preferred_element_type=F32), 0.0)
            s_prev = s_scr[h]
            o = _dot(qg, s_prev.astype(BF16)) + _dot(a.astype(BF16), v)
            s_scr[h] = decay_all[dk, :] * s_prev + lax.dot_general(kd_all[:, dk], v, _TN,
                                                                   preferred_element_type=F32)
            o_ref[rows, dv] = _rms(o, g_ref[...])

    @pl.when(t == pl.num_programs(1) - 1)
    def _():
        sout_ref[...] = s_scr[...]


def _gla(qb, kb, vb, lg, s0, g, *, batch, seq, chunk):
    tb = min(seq, GLA_TILE)
    assert seq % tb == 0 and tb % chunk == 0
    nt = seq // tb
    tok = lambda b, t: (b * nt + t, 0)
    st = lambda b, t: (b, 0, 0, 0)
    return pl.pallas_call(
        functools.partial(_gla_kernel, chunk=chunk),
        grid=(batch, nt),
        in_specs=[
            pl.BlockSpec((tb, H_B * DK_B), tok),
            pl.BlockSpec((tb, H_B * DK_B), tok),
            pl.BlockSpec((tb, W_B), tok),
            pl.BlockSpec((tb, H_B * DK_B), tok),
            pl.BlockSpec((None, H_B, DK_B, DV_B), st),
            pl.BlockSpec(g.shape, lambda b, t: (0, 0)),
        ],
        out_specs=[
            pl.BlockSpec((tb, W_B), tok),
            pl.BlockSpec((None, H_B, DK_B, DV_B), st),
        ],
        out_shape=[
            jax.ShapeDtypeStruct((batch * seq, W_B), F32),
            jax.ShapeDtypeStruct((batch, H_B, DK_B, DV_B), F32),
        ],
        scratch_shapes=[pltpu.VMEM((H_B, DK_B, DV_B), F32)],
        compiler_params=_cparams("parallel", "arbitrary"),
        name="gla",
    )(qb, kb, vb, lg, s0, g)


def _mix_kernel(x_ref, oa_ref, ob_ref, gb_ref, wo_ref, g1_ref, b1_ref, wr_ref, br_ref,
                x1_ref, gw_ref, idx_ref, cnt_ref, cnt_scr):
    i = pl.program_id(0)

    @pl.when(i == 0)
    def _():
        cnt_scr[...] = jnp.zeros(cnt_scr.shape, F32)

    gb = gb_ref[...]
    obg = ob_ref[...] * (gb * jax.nn.sigmoid(gb))
    mix = (_dot(oa_ref[...].astype(BF16), wo_ref[0:W_A, :])
           + _dot(obg.astype(BF16), wo_ref[W_A:D_MODEL, :]))
    x1 = _layernorm(ALPHA * x_ref[...] + mix, g1_ref[...], b1_ref[...])
    x1_ref[...] = x1

    logits = _dot(x1, wr_ref[...], HI) + br_ref[...]
    tm = logits.shape[0]
    lane = lax.broadcasted_iota(I32, logits.shape, 1)
    lane_f = lane.astype(F32)
    big = float(LANES)
    neg = -jnp.inf

    def first_argmax(v, vmax):
        return jnp.min(jnp.where(v == vmax, lane_f, big), axis=1, keepdims=True)

    lg1 = jnp.where((lane >= N_EXPERTS) & (lane < N_EXPERTS + N_GROUPS), logits, neg)
    m1 = jnp.max(lg1, axis=1, keepdims=True)
    pg = 1.0 / jnp.sum(jnp.exp(lg1 - m1), axis=1, keepdims=True)
    grp = first_argmax(lg1, m1) - float(N_EXPERTS)
    in_grp = (lane_f >= grp * EXP_PER_GROUP) & (lane_f < (grp + 1.0) * EXP_PER_GROUP)
    lg2 = jnp.where(in_grp, logits, neg)
    v1 = jnp.max(lg2, axis=1, keepdims=True)
    i1 = first_argmax(lg2, v1)
    lg2b = jnp.where(lane_f == i1, neg, lg2)
    v2 = jnp.max(lg2b, axis=1, keepdims=True)
    i2 = first_argmax(lg2b, v2)
    t = jnp.exp(v2 - v1)
    w1 = pg / (1.0 + t)
    w2 = pg * t / (1.0 + t)

    hit1 = lane_f == i1
    hit2 = lane_f == i2
    sel = jnp.where(hit1 | hit2, 1.0, 0.0)
    r = lax.broadcasted_iota(I32, (tm, tm), 0)
    c = lax.broadcasted_iota(I32, (tm, tm), 1)
    earlier = jnp.where(c < r, 1.0, 0.0).astype(BF16)
    rank = _dot(earlier, sel.astype(BF16)) + cnt_scr[...]
    pos1 = jnp.sum(jnp.where(hit1, rank, 0.0), axis=1, keepdims=True)
    pos2 = jnp.sum(jnp.where(hit2, rank, 0.0), axis=1, keepdims=True)
    cnt = cnt_scr[...] + jnp.sum(sel, axis=0, keepdims=True)
    cnt_scr[...] = cnt
    cnt_ref[...] = cnt.astype(I32)

    gw_ref[...] = jnp.where(lane == 0, w1, jnp.where(lane == 1, w2, 0.0))
    idx_f = jnp.where(lane == 0, i1, jnp.where(lane == 1, i2, jnp.where(lane == 2, pos1,
                      jnp.where(lane == 3, pos2, 0.0))))
    idx_ref[...] = idx_f.astype(I32)


def _mix(x, oa, ob, gb, wo, g1, b1, wr, br):
    n = x.shape[0]
    tm = TOKEN_TILE
    row = lambda i: (i, 0)
    fixed = lambda i: (0, 0)
    return pl.pallas_call(
        _mix_kernel,
        grid=(n // tm,),
        in_specs=[
            pl.BlockSpec((tm, D_MODEL), row),
            pl.BlockSpec((tm, W_A), row),
            pl.BlockSpec((tm, W_B), row),
            pl.BlockSpec((tm, W_B), row),
            pl.BlockSpec(wo.shape, fixed),
            pl.BlockSpec(g1.shape, fixed),
            pl.BlockSpec(b1.shape, fixed),
            pl.BlockSpec(wr.shape, fixed),
            pl.BlockSpec(br.shape, fixed),
        ],
        out_specs=[
            pl.BlockSpec((tm, D_MODEL), row),
            pl.BlockSpec((tm, LANES), row),
            pl.BlockSpec((tm, LANES), row),
            pl.BlockSpec((1, LANES), fixed),
        ],
        out_shape=[
            jax.ShapeDtypeStruct((n, D_MODEL), F32),
            jax.ShapeDtypeStruct((n, LANES), F32),
            jax.ShapeDtypeStruct((n, LANES), I32),
            jax.ShapeDtypeStruct((1, LANES), I32),
        ],
        scratch_shapes=[pltpu.VMEM((1, LANES), F32)],
        compiler_params=_cparams("arbitrary"),
        name="mix_route",
    )(x, oa, ob, gb, wo, g1, b1, wr, br)


def _row_copy(src, dst, sem):
    return pltpu.make_async_copy(src, dst, sem)


def _dispatch_kernel(pstart_ref, idx_ref, x_ref, buf_in_ref, buf_ref, sem):
    del buf_in_ref
    tm = x_ref.shape[0]

    def issue(t, carry):
        for k in range(2):
            dst = pstart_ref[idx_ref[0, 0, 4 * t + k]] + idx_ref[0, 0, 4 * t + 2 + k]
            _row_copy(x_ref.at[pl.ds(t, 1), :], buf_ref.at[pl.ds(dst, 1), :], sem).start()
        return carry

    lax.fori_loop(0, tm, issue, 0)

    def drain(t, carry):
        for k in range(2):
            _row_copy(x_ref.at[pl.ds(0, 1), :], buf_ref.at[pl.ds(0, 1), :], sem).wait()
        return carry

    lax.fori_loop(0, tm, drain, 0)


def _dispatch(pstart, idx, x1, n_rows):
    n = x1.shape[0]
    tm = TOKEN_TILE
    grid_spec = pltpu.PrefetchScalarGridSpec(
        num_scalar_prefetch=1,
        grid=(n // tm,),
        in_specs=[
            pl.BlockSpec((1, 1, 4 * tm), lambda i, ps: (i, 0, 0), memory_space=pltpu.SMEM),
            pl.BlockSpec((tm, D_MODEL), lambda i, ps: (i, 0)),
            pl.BlockSpec(memory_space=pl.ANY),
        ],
        out_specs=pl.BlockSpec(memory_space=pl.ANY),
        scratch_shapes=[pltpu.SemaphoreType.DMA(())],
    )
    return pl.pallas_call(
        _dispatch_kernel,
        grid_spec=grid_spec,
        out_shape=jax.ShapeDtypeStruct((n_rows, D_MODEL), F32),
        input_output_aliases={3: 0},
        compiler_params=_cparams("arbitrary"),
        name="dispatch",
    )(pstart, idx, x1, jnp.zeros((n_rows, D_MODEL), F32))


def _expert_kernel(be_ref, nu_ref, x_ref, wg_ref, wu_ref, wd_ref, y_ref, wgb, wub, wdb):
    j = pl.program_id(0)

    @pl.when(j < nu_ref[0])
    def _():
        prev = be_ref[jnp.maximum(j - 1, 0)]

        @pl.when((j == 0) | (be_ref[j] != prev))
        def _():
            wgb[...] = wg_ref[...].astype(BF16)
            wub[...] = wu_ref[...].astype(BF16)
            wdb[...] = wd_ref[...].astype(BF16)

        x = x_ref[...].astype(BF16)
        hg = _dot(x, wgb[...])
        hu = _dot(x, wub[...])
        h = hg * jax.nn.sigmoid(hg) * hu
        y_ref[...] = _dot(h.astype(BF16), wdb[...])

    @pl.when(j >= nu_ref[0])
    def _():
        y_ref[...] = jnp.zeros(y_ref.shape, F32)


def _experts(block_e, n_used, xbuf, w_gate, w_up, w_down, *, li):
    n_rows = xbuf.shape[0]
    blk = EXPERT_BLOCK
    nb = n_rows // blk

    def rows(j, be, nu):
        return (jnp.minimum(j, nu[0] - 1), 0)

    def wsel(j, be, nu):
        return (li, be[jnp.minimum(j, nu[0] - 1)], 0, 0)

    grid_spec = pltpu.PrefetchScalarGridSpec(
        num_scalar_prefetch=2,
        grid=(nb,),
        in_specs=[
            pl.BlockSpec((blk, D_MODEL), rows),
            pl.BlockSpec((None, None, D_MODEL, D_EXPERT), wsel),
            pl.BlockSpec((None, None, D_MODEL, D_EXPERT), wsel),
            pl.BlockSpec((None, None, D_EXPERT, D_MODEL), wsel),
        ],
        out_specs=pl.BlockSpec((blk, D_MODEL), lambda j, be, nu: (j, 0)),
        scratch_shapes=[
            pltpu.VMEM((D_MODEL, D_EXPERT), BF16),
            pltpu.VMEM((D_MODEL, D_EXPERT), BF16),
            pltpu.VMEM((D_EXPERT, D_MODEL), BF16),
        ],
    )
    return pl.pallas_call(
        _expert_kernel,
        grid_spec=grid_spec,
        out_shape=jax.ShapeDtypeStruct((n_rows, D_MODEL), F32),
        compiler_params=_cparams("arbitrary"),
        name="experts",
    )(block_e, n_used, xbuf, w_gate, w_up, w_down)


def _combine_kernel(pstart_ref, idx_ref, gw_ref, x1_ref, p_ref, ybuf_ref, wpp_ref, wpg_ref,
                    g2_ref, b2_ref, gp_ref, o_ref, y_scr, sem):
    tm = x1_ref.shape[0]

    def issue(t, carry):
        for k in range(2):
            src = pstart_ref[idx_ref[0, 0, 4 * t + k]] + idx_ref[0, 0, 4 * t + 2 + k]
            _row_copy(ybuf_ref.at[pl.ds(src, 1), :], y_scr.at[k, pl.ds(t, 1), :], sem).start()
        return carry

    lax.fori_loop(0, tm, issue, 0)

    def drain(t, carry):
        for k in range(2):
            _row_copy(ybuf_ref.at[pl.ds(0, 1), :], y_scr.at[k, pl.ds(0, 1), :], sem).wait()
        return carry

    lax.fori_loop(0, tm, drain, 0)

    gw = gw_ref[...]
    y = gw[:, 0:1] * y_scr[0] + gw[:, 1:2] * y_scr[1]
    x2 = _layernorm(ALPHA * x1_ref[...] + y, g2_ref[...], b2_ref[...])
    e = _rms(_dot(p_ref[...].astype(BF16), wpp_ref[...]), gp_ref[...])
    o_ref[...] = x2 + jax.nn.sigmoid(_dot(x2.astype(BF16), wpg_ref[...])) * e


def _combine(pstart, idx, gw, x1, p_l, ybuf, wpp, wpg, g2, b2, gp):
    n = x1.shape[0]
    tm = TOKEN_TILE
    row = lambda i, ps: (i, 0)
    fixed = lambda i, ps: (0, 0)
    grid_spec = pltpu.PrefetchScalarGridSpec(
        num_scalar_prefetch=1,
        grid=(n // tm,),
        in_specs=[
            pl.BlockSpec((1, 1, 4 * tm), lambda i, ps: (i, 0, 0), memory_space=pltpu.SMEM),
            pl.BlockSpec((tm, LANES), row),
            pl.BlockSpec((tm, D_MODEL), row),
            pl.BlockSpec((tm, D_PLE), row),
            pl.BlockSpec(memory_space=pl.ANY),
            pl.BlockSpec(wpp.shape, fixed),
            pl.BlockSpec(wpg.shape, fixed),
            pl.BlockSpec(g2.shape, fixed),
            pl.BlockSpec(b2.shape, fixed),
            pl.BlockSpec(gp.shape, fixed),
        ],
        out_specs=pl.BlockSpec((tm, D_MODEL), row),
        scratch_shapes=[pltpu.VMEM((2, tm, D_MODEL), F32), pltpu.SemaphoreType.DMA(())],
    )
    return pl.pallas_call(
        _combine_kernel,
        grid_spec=grid_spec,
        out_shape=jax.ShapeDtypeStruct((n, D_MODEL), F32),
        compiler_params=_cparams("arbitrary"),
        name="combine",
    )(pstart, idx, gw, x1, p_l, ybuf, wpp, wpg, g2, b2, gp)


def _rope_tables(pos):
    half = HD_A // 2
    inv = ROPE_THETA ** (-jnp.arange(half, dtype=F32) / half)
    ang = pos.astype(F32)[:, None] * inv[None, :]
    cos, sin = jnp.cos(ang), jnp.sin(ang)
    reps = LANES // HD_A
    return (jnp.tile(jnp.concatenate([cos, cos], axis=1), (1, reps)),
            jnp.tile(jnp.concatenate([-sin, sin], axis=1), (1, reps)))


def _moe_plan(counts, n_tokens):
    blk = EXPERT_BLOCK
    nb = (2 * n_tokens) // blk + N_EXPERTS
    padded = (counts + blk - 1) // blk * blk
    pends = jnp.cumsum(padded)
    pstart = (pends - padded).astype(I32)
    first_row = jnp.arange(nb, dtype=I32) * blk
    block_e = jnp.minimum(jnp.sum(pends[None, :] <= first_row[:, None], axis=1), N_EXPERTS - 1).astype(I32)
    n_used = (pends[-1:] // blk).astype(I32)
    return pstart, block_e, n_used, nb * blk


def _layer(x, p_l, li, lw, rope, attend, s0, *, batch, seq, chunk):
    cos_t, sin_t = rope
    qa, ka, va, qb, kb, vb, gb, lg, vt = _inproj(x, lw["w_main"], lw["w_vt"], lw["w_glr"], lw["w_gk2"],
                                                 lw["b_gk2"], cos_t, sin_t)
    oa = attend(qa, ka, va, vt)
    ob, s_out = _gla(qb, kb, vb, lg, s0, lw["gla_g"], batch=batch, seq=seq, chunk=chunk)
    x1, gw, idx, cnt = _mix(x, oa, ob, gb, lw["w_o"], lw["ln1_g"], lw["ln1_b"], lw["w_r"], lw["b_r"])
    n = x.shape[0]
    pstart, block_e, n_used, n_rows = _moe_plan(cnt[0, :N_EXPERTS], n)
    idx4 = idx[:, :4].reshape(n // TOKEN_TILE, 1, 4 * TOKEN_TILE)
    xbuf = _dispatch(pstart, idx4, x1, n_rows)
    ybuf = _experts(block_e, n_used, xbuf, lw["w_gate"], lw["w_up"], lw["w_down"], li=li)
    out = _combine(pstart, idx4, gw, x1, p_l, ybuf, lw["w_pp"], lw["w_pg"],
                   lw["ln2_g"], lw["ln2_b"], lw["ple_g"])
    return out, ka, va, s_out


def kernel(x_prompt, x_sample, cache_k, cache_v, state_gla, page_table, p_prompt, p_sample, w_in, w_gk2, b_gk2, lam_q1, lam_k1, lam_q2, lam_k2, diff_norm_g, gla_norm_g, w_o, ln1_g, ln1_b, w_r1, b_r1, w_r2, b_r2, w_gate, w_up, w_down, ln2_g, ln2_b, w_ple_gate, w_ple_proj, ple_norm_g):
    bp, tp, _ = x_prompt.shape
    bs, ts, _ = x_sample.shape
    past = page_table.shape[1] * PAGE_SIZE
    rope_p = _rope_tables(jnp.arange(tp))
    rope_s = tuple(jnp.tile(t, (TOKEN_TILE // ts, 1)) for t in _rope_tables(past + jnp.arange(ts)))

    row2 = lambda a: a.reshape(1, -1)
    yp = x_prompt.reshape(bp * tp, D_MODEL)
    ys = x_sample.reshape(bs * ts, D_MODEL)
    s0_p = jnp.zeros((bp, H_B, DK_B, DV_B), F32)
    outs = {k: [] for k in ("kp", "vp", "sp", "ks", "vs", "ss")}
    for li in range(DEPTH):
        lw = {
            "w_main": w_in[li, :, :_OFF_GLR].astype(BF16),
            "w_vt": jnp.transpose(w_in[li, :, _OFF_VA:_OFF_VA + W_A]).astype(BF16),
            "w_glr": jnp.pad(w_in[li, :, _OFF_GLR:], ((0, 0), (0, LANES - GATE_RANK))).astype(BF16),
            "w_gk2": jnp.pad(w_gk2[li], ((0, LANES - GATE_RANK), (0, 0))),
            "b_gk2": row2(b_gk2[li]),
            "gla_g": row2(gla_norm_g[li]),
            "w_o": w_o[li].astype(BF16),
            "ln1_g": row2(ln1_g[li]), "ln1_b": row2(ln1_b[li]),
            "w_r": jnp.pad(jnp.concatenate([w_r2[li], w_r1[li]], axis=1),
                           ((0, 0), (0, LANES - N_EXPERTS - N_GROUPS))),
            "b_r": row2(jnp.pad(jnp.concatenate([b_r2[li], b_r1[li]]), (0, LANES - N_EXPERTS - N_GROUPS))),
            "w_gate": w_gate, "w_up": w_up, "w_down": w_down,
            "w_pp": w_ple_proj[li].astype(BF16),
            "w_pg": w_ple_gate[li].astype(BF16),
            "ln2_g": row2(ln2_g[li]), "ln2_b": row2(ln2_b[li]),
            "ple_g": row2(ple_norm_g[li]),
        }
        lamq = jnp.stack([lam_q1[li], lam_q2[li]])
        lamk = jnp.stack([lam_k1[li], lam_k2[li]])
        dg = row2(diff_norm_g[li])

        def attend_p(qa, ka, va, vt, lamq=lamq, lamk=lamk, dg=dg, li=li):
            del va
            return _prompt_attention(qa, ka, vt, lamq, lamk, dg, li=li, batch=bp, seq=tp)

        yp, ka, va, s = _layer(yp, p_prompt[li].reshape(bp * tp, D_PLE), li, lw, rope_p, attend_p, s0_p,
                               batch=bp, seq=tp, chunk=GLA_CHUNK)
        outs["kp"].append(ka.reshape(bp, tp, H_A, DKV_A))
        outs["vp"].append(va.reshape(bp, tp, H_A, DKV_A))
        outs["sp"].append(s)

        def attend_s(qa, ka, va, vt, lamq=lamq, lamk=lamk, dg=dg, li=li):
            del vt
            return _sample_attention(qa, ka, va, cache_k, cache_v, page_table, lamq, lamk, dg,
                                     li=li, batch=bs, ts=ts)

        ys, ka, va, s = _layer(ys, p_sample[li].reshape(bs * ts, D_PLE), li, lw, rope_s, attend_s,
                               state_gla[:, li], batch=bs, seq=ts, chunk=ts)
        outs["ks"].append(ka.reshape(bs, ts, H_A, DKV_A))
        outs["vs"].append(va.reshape(bs, ts, H_A, DKV_A))
        outs["ss"].append(s)

    stack = lambda k: jnp.stack(outs[k], axis=1)
    return (yp.reshape(bp, tp, D_MODEL), ys.reshape(bs, ts, D_MODEL),
            stack("kp"), stack("vp"), stack("sp"), stack("ks"), stack("vs"), stack("ss"))
```

```python
import functools
import math

import jax
import jax.numpy as jnp
from jax import lax
from jax.experimental import pallas as pl
from jax.experimental.pallas import tpu as pltpu

F32 = jnp.float32
BF16 = jnp.bfloat16
I32 = jnp.int32
HI = lax.Precision.HIGHEST

D_MODEL = 1024
DEPTH = 2
PAGE_SIZE = 128
D_PLE = 256
HD_A = 64
DKV_A = 2 * HD_A
W_A = D_MODEL // 2
H_A = W_A // DKV_A
W_B = D_MODEL - W_A
H_B = 4
DV_B = W_B // H_B
DK_B = DV_B // 2
GATE_RANK = 16
GATE_NORM = 16.0
GLA_CHUNK = 64
N_GROUPS = 4
EXP_PER_GROUP = 8
N_EXPERTS = N_GROUPS * EXP_PER_GROUP
D_EXPERT = D_MODEL // 2
ROPE_THETA = 10000.0
ALPHA = (2 * DEPTH) ** 0.25
EPS = 1e-5
LOG2E = math.log2(math.e)

LANES = 128
VMEM_LIMIT = 48 * 1024 * 1024

TOKEN_TILE = 256
ATTN_TILE = 256
PAGES_PER_STEP = 8
GLA_TILE = 256
EXPERT_BLOCK = 256
MIX_ROWS = 128

_OFF_QA, _OFF_KA, _OFF_VA = 0, W_A, 2 * W_A
_OFF_QB = 3 * W_A
_OFF_KB = _OFF_QB + H_B * DK_B
_OFF_VB = _OFF_KB + H_B * DK_B
_OFF_GB = _OFF_VB + W_B
_OFF_GLR = _OFF_GB + W_B

_NT = (((1,), (1,)), ((), ()))
_TN = (((0,), (0,)), ((), ()))


def _cparams(*sem):
    return pltpu.CompilerParams(dimension_semantics=sem, vmem_limit_bytes=VMEM_LIMIT)


def _dot(a, b, precision=None):
    return jnp.dot(a, b, preferred_element_type=F32, precision=precision)


def _lam_init(li):
    return 0.8 - 0.6 * math.exp(-0.3 * li)


def _lam(lamq_ref, lamk_ref, li):
    s = jnp.sum(lamq_ref[...] * lamk_ref[...], axis=1, keepdims=True)
    e = jnp.exp(s)
    return e[0:1, :] - e[1:2, :] + _lam_init(li)


def _rms(x, g):
    return x * lax.rsqrt(jnp.mean(x * x, axis=-1, keepdims=True) + EPS) * g


def _layernorm(x, g, b):
    mu = jnp.mean(x, axis=-1, keepdims=True)
    xc = x - mu
    var = jnp.mean(xc * xc, axis=-1, keepdims=True)
    return xc * lax.rsqrt(var + EPS) * g + b


def _inproj_kernel(x_ref, w_ref, wvt_ref, wglr_ref, wgk2_ref, bgk2_ref, cos_ref, sin_ref, *refs):
    qa_ref, ka_ref, qb_ref, kb_ref, vb_ref, gb_ref, lg_ref, vt_ref, k4_ref, v4_ref = refs[-10:]
    tm = x_ref.shape[0]
    layer = k4_ref.shape[0] - 1
    if len(refs) > 10:
        k4_ref[0:layer] = refs[0][...]
        v4_ref[0:layer] = refs[1][...]
    xb = x_ref[...].astype(BF16)
    vt_ref[...] = lax.dot_general(wvt_ref[...], xb, _NT, preferred_element_type=F32).astype(BF16)
    cos = cos_ref[...]
    sin = sin_ref[...]
    lane = lax.broadcasted_iota(I32, cos.shape, 1)
    first_half = (lane & (HD_A // 2)) == 0

    def rope(h):
        partner = jnp.where(first_half, pltpu.roll(h, LANES - HD_A // 2, 1), pltpu.roll(h, HD_A // 2, 1))
        return h * cos + partner * sin

    for c in range(W_A // LANES):
        lo = c * LANES
        hq = _dot(xb, w_ref[:, _OFF_QA + lo:_OFF_QA + lo + LANES])
        qa_ref[:, lo:lo + LANES] = rope(hq) * (HD_A ** -0.5 * LOG2E)
        hk = rope(_dot(xb, w_ref[:, _OFF_KA + lo:_OFF_KA + lo + LANES]))
        ka_ref[:, lo:lo + LANES] = hk
        k4_ref[layer, pl.ds(c, tm, stride=H_A), :] = hk
        v4_ref[layer, pl.ds(c, tm, stride=H_A), :] = _dot(xb, w_ref[:, _OFF_VA + lo:_OFF_VA + lo + LANES])
    qb_ref[...] = _dot(xb, w_ref[:, _OFF_QB:_OFF_KB]) * (DK_B ** -0.5)
    kb_ref[...] = _dot(xb, w_ref[:, _OFF_KB:_OFF_VB])
    vb_ref[...] = _dot(xb, w_ref[:, _OFF_VB:_OFF_GB])
    gb_ref[...] = _dot(xb, w_ref[:, _OFF_GB:_OFF_GLR])
    glr = _dot(xb, wglr_ref[...])
    z = _dot(glr, wgk2_ref[...], HI) + bgk2_ref[...]
    lg_ref[...] = (jnp.minimum(z, 0.0) - jnp.log1p(jnp.exp(-jnp.abs(z)))) * (1.0 / GATE_NORM)


def _inproj(x, w_main, w_vt, w_glr, w_gk2p, b_gk2, cos_t, sin_t, *, groups, kv_prev=None):
    n = x.shape[0]
    tm = TOKEN_TILE
    ntab = cos_t.shape[0] // tm
    nq = n // tm // groups
    kv_idx = lambda i: (i // nq, 0, i % nq, 0)
    layers = 1 if kv_prev is None else kv_prev[0].shape[1] + 1
    row = lambda i: (i, 0)
    fixed = lambda i: (0, 0)
    tab = lambda i: (i % ntab, 0)
    widths = (W_A, W_A, H_B * DK_B, H_B * DK_B, W_B, W_B, H_B * DK_B)
    operands = [x, w_main, w_vt, w_glr, w_gk2p, b_gk2, cos_t, sin_t]
    in_specs = [
        pl.BlockSpec((tm, D_MODEL), row),
        pl.BlockSpec(w_main.shape, fixed),
        pl.BlockSpec(w_vt.shape, fixed),
        pl.BlockSpec(w_glr.shape, fixed),
        pl.BlockSpec(w_gk2p.shape, fixed),
        pl.BlockSpec(b_gk2.shape, fixed),
        pl.BlockSpec((tm, LANES), tab),
        pl.BlockSpec((tm, LANES), tab),
    ]
    if kv_prev is not None:
        operands += list(kv_prev)
        in_specs += [pl.BlockSpec((None, layers - 1, tm * H_A, DKV_A), kv_idx)] * 2
    kv_shape = jax.ShapeDtypeStruct((groups, layers, nq * tm * H_A, DKV_A), F32)
    return pl.pallas_call(
        _inproj_kernel,
        grid=(n // tm,),
        in_specs=in_specs,
        out_specs=[pl.BlockSpec((tm, w), row) for w in widths]
        + [pl.BlockSpec((None, W_A, tm), lambda i: (i, 0, 0))]
        + [pl.BlockSpec((None, layers, tm * H_A, DKV_A), kv_idx)] * 2,
        out_shape=[jax.ShapeDtypeStruct((n, w), F32) for w in widths]
        + [jax.ShapeDtypeStruct((n // tm, W_A, tm), BF16), kv_shape, kv_shape],
        compiler_params=_cparams("parallel"),
        name="inproj",
    )(*operands)


def _pattn_kernel(lamq_ref, lamk_ref, g_ref, q_ref, k_ref, vt_ref, o_ref, acc_scr, sa_scr, sb_scr, *, li):
    tq = q_ref.shape[0]
    win = 2 * tq
    last_win = k_ref.shape[0] // win - 1
    qi = pl.program_id(2)
    n_pairs = lax.shift_right_logical(qi, 2)
    q = q_ref[...]
    lane = lax.broadcasted_iota(I32, q.shape, 1)
    qc = (jnp.where(lane < HD_A, q, 0.0).astype(BF16), jnp.where(lane >= HD_A, q, 0.0).astype(BF16))
    acc_scr[...] = jnp.zeros(acc_scr.shape, F32)

    def scores(w, s_scr):
        w = jnp.minimum(w, last_win)
        start = pl.multiple_of(w * win, win)
        kb = k_ref[pl.ds(start, win), :].astype(BF16)
        for c in range(2):
            s_scr[c] = lax.dot_general(kb, qc[c], _NT, preferred_element_type=F32)

    def consume(w, s_scr, stats, masked):
        vt0 = vt_ref[2 * w]
        vt1 = vt_ref[2 * w + 1]
        out = []
        for c in range(2):
            m_prev, l_prev = stats[c]
            st = s_scr[c]
            if masked:
                key = lax.broadcasted_iota(I32, st.shape, 0)
                qry = lax.broadcasted_iota(I32, (1, tq), 1) + (qi * tq - w * win)
                st = jnp.where(key <= qry, st, -jnp.inf)
            m_new = jnp.maximum(m_prev, jnp.max(st, axis=0, keepdims=True))
            alpha = jnp.exp2(m_prev - m_new)
            p = jnp.exp2(st - m_new)
            l_new = alpha * l_prev + jnp.sum(p, axis=0, keepdims=True)
            p = p.astype(BF16)
            cols = slice(c * tq, (c + 1) * tq)
            acc_scr[:, cols] = alpha * acc_scr[:, cols] + (_dot(vt0, p[0:tq]) + _dot(vt1, p[tq:win]))
            out.append((m_new, l_new))
        return tuple(out)

    def pair(i, stats):
        scores(2 * i + 1, sb_scr)
        stats = consume(2 * i, sa_scr, stats, False)
        scores(2 * i + 2, sa_scr)
        return consume(2 * i + 1, sb_scr, stats, False)

    def last_pair(i, stats):
        scores(2 * i + 1, sb_scr)
        stats = consume(2 * i, sa_scr, stats, True)
        return consume(2 * i + 1, sb_scr, stats, True)

    scores(0, sa_scr)
    stat = (jnp.full((1, tq), -jnp.inf, F32), jnp.zeros((1, tq), F32))
    stats = lax.fori_loop(0, n_pairs, pair, (stat, stat))
    (_, l1), (_, l2) = last_pair(n_pairs, stats)
    out_t = acc_scr[:, 0:tq] / l1 - _lam(lamq_ref, lamk_ref, li) * (acc_scr[:, tq:2 * tq] / l2)
    o_ref[...] = _rms(jnp.transpose(out_t), g_ref[...]) * (1.0 - _lam_init(li))


def _prompt_attention(qa, ka, vt, lamq, lamk, g, *, li, batch, seq):
    tq = ATTN_TILE
    assert tq == TOKEN_TILE and seq % (4 * tq) == 0
    nq = seq // tq
    small = lambda b, h, i: (0, 0)
    return pl.pallas_call(
        functools.partial(_pattn_kernel, li=li),
        grid=(batch, H_A, nq),
        in_specs=[
            pl.BlockSpec(lamq.shape, small),
            pl.BlockSpec(lamk.shape, small),
            pl.BlockSpec(g.shape, small),
            pl.BlockSpec((tq, DKV_A), lambda b, h, i: (b * nq + i, h)),
            pl.BlockSpec((seq, DKV_A), lambda b, h, i: (b, h)),
            pl.BlockSpec((nq, DKV_A, tq), lambda b, h, i: (b, h, 0)),
        ],
        out_specs=pl.BlockSpec((tq, DKV_A), lambda b, h, i: (b * nq + i, h)),
        out_shape=jax.ShapeDtypeStruct(qa.shape, F32),
        scratch_shapes=[pltpu.VMEM((DKV_A, 2 * tq), F32),
                        pltpu.VMEM((2, 2 * tq, tq), F32),
                        pltpu.VMEM((2, 2 * tq, tq), F32)],
        compiler_params=_cparams("parallel", "parallel", "arbitrary"),
        name="prompt_attn",
    )(lamq, lamk, g, qa, ka, vt)


def _sattn_kernel(pt_ref, lamq_ref, lamk_ref, g_ref, q_ref, kn_ref, vn_ref, *refs, li, npages, ts):
    k_pages = refs[:npages]
    v_pages = refs[npages:2 * npages]
    o_ref = refs[2 * npages]
    qt_scr, m_scr, l_scr, acc_scr, kn_scr, vn_scr = refs[2 * npages + 1:]
    step = pl.program_id(1)
    nrow = H_A * 2 * ts
    rows_per_page = PAGE_SIZE * H_A

    def head_match(shape):
        row = lax.broadcasted_iota(I32, shape, 0)
        col = lax.broadcasted_iota(I32, shape, 1)
        return (col & (H_A - 1)) == (row // (2 * ts)), row, col

    def update(state, scores, values):
        m_prev, l_prev, acc = state
        m_new = m_prev
        for s in scores:
            m_new = jnp.maximum(m_new, jnp.max(s, axis=1, keepdims=True))
        alpha = jnp.exp2(m_prev - m_new)
        l_new = alpha * l_prev
        acc = alpha * acc
        for s, v_bf in zip(scores, values):
            p = jnp.exp2(s - m_new)
            l_new = l_new + jnp.sum(p, axis=1, keepdims=True)
            acc = acc + _dot(p.astype(BF16), v_bf)
        return m_new, l_new, acc

    @pl.when(step == 0)
    def _():
        q = q_ref[...]
        lane = lax.broadcasted_iota(I32, (ts, DKV_A), 1)
        blocks = []
        for h in range(H_A):
            qh = q[:, h * DKV_A:(h + 1) * DKV_A]
            blocks += [jnp.where(lane < HD_A, qh, 0.0), jnp.where(lane >= HD_A, qh, 0.0)]
        qt = jnp.concatenate(blocks, axis=0).astype(BF16)
        qt_scr[...] = qt
        kn_scr[...] = jnp.zeros(kn_scr.shape, F32)
        vn_scr[...] = jnp.zeros(vn_scr.shape, F32)
        kn_scr[0:ts * H_A, :] = kn_ref[...]
        vn_scr[0:ts * H_A, :] = vn_ref[...]
        s = lax.dot_general(qt, kn_scr[...].astype(BF16), _NT, preferred_element_type=F32)
        match, row, col = head_match(s.shape)
        ok = match & ((col // H_A) <= (row & (ts - 1))) & (col < ts * H_A)
        s = jnp.where(ok, s, -jnp.inf)
        init = (jnp.full((nrow, 1), -jnp.inf, F32), jnp.zeros((nrow, 1), F32), jnp.zeros((nrow, DKV_A), F32))
        m, l, acc = update(init, [s], [vn_scr[...].astype(BF16)])
        m_scr[...] = jnp.broadcast_to(m, m_scr.shape)
        l_scr[...] = jnp.broadcast_to(l, l_scr.shape)
        acc_scr[...] = acc

    qt = qt_scr[...]
    match, _, _ = head_match((nrow, rows_per_page))
    scores = [jnp.where(match, lax.dot_general(qt, kp[...].astype(BF16), _NT, preferred_element_type=F32),
                        -jnp.inf) for kp in k_pages]
    state = (m_scr[:, 0:1], l_scr[:, 0:1], acc_scr[...])
    half = npages // 2
    for grp in (slice(0, half), slice(half, npages)):
        state = update(state, scores[grp], [vp[...].astype(BF16) for vp in v_pages[grp]])
    m, l, acc = state
    m_scr[...] = jnp.broadcast_to(m, m_scr.shape)
    l_scr[...] = jnp.broadcast_to(l, l_scr.shape)
    acc_scr[...] = acc

    @pl.when(step == pl.num_programs(1) - 1)
    def _():
        o = acc_scr[...] / l_scr[:, 0:1]
        lam = _lam(lamq_ref, lamk_ref, li)
        for h in range(H_A):
            r0 = h * 2 * ts
            out = o[r0:r0 + ts, :] - lam * o[r0 + ts:r0 + 2 * ts, :]
            o_ref[:, h * DKV_A:(h + 1) * DKV_A] = _rms(out, g_ref[...]) * (1.0 - _lam_init(li))


def _sample_attention(qa, kn, vn, cache_k, cache_v, page_table, lamq, lamk, g, *, li, batch, ts):
    npg = PAGES_PER_STEP
    n_pages = page_table.shape[1]
    assert n_pages % npg == 0 and ts == 8 and ts * H_A <= PAGE_SIZE
    nrow = H_A * 2 * ts
    rows_per_page = PAGE_SIZE * H_A
    small = lambda b, s, pt: (0, 0)
    new = lambda b, s, pt: (b, 0)

    def page_spec(i):
        return pl.BlockSpec((None, None, rows_per_page, DKV_A),
                            lambda b, s, pt, i=i: (pt[b, s * npg + i], li, 0, 0))

    grid_spec = pltpu.PrefetchScalarGridSpec(
        num_scalar_prefetch=1,
        grid=(batch, n_pages // npg),
        in_specs=[
            pl.BlockSpec(lamq.shape, small),
            pl.BlockSpec(lamk.shape, small),
            pl.BlockSpec(g.shape, small),
            pl.BlockSpec((ts, W_A), new),
            pl.BlockSpec((ts * H_A, DKV_A), new),
            pl.BlockSpec((ts * H_A, DKV_A), new),
        ] + [page_spec(i) for i in range(npg)] + [page_spec(i) for i in range(npg)],
        out_specs=pl.BlockSpec((ts, W_A), new),
        scratch_shapes=[
            pltpu.VMEM((nrow, DKV_A), BF16),
            pltpu.VMEM((nrow, LANES), F32),
            pltpu.VMEM((nrow, LANES), F32),
            pltpu.VMEM((nrow, DKV_A), F32),
            pltpu.VMEM((PAGE_SIZE, DKV_A), F32),
            pltpu.VMEM((PAGE_SIZE, DKV_A), F32),
        ],
    )
    ck = cache_k.reshape(cache_k.shape[0], DEPTH, rows_per_page, DKV_A)
    cv = cache_v.reshape(cache_v.shape[0], DEPTH, rows_per_page, DKV_A)
    return pl.pallas_call(
        functools.partial(_sattn_kernel, li=li, npages=npg, ts=ts),
        grid_spec=grid_spec,
        out_shape=jax.ShapeDtypeStruct(qa.shape, F32),
        compiler_params=_cparams("parallel", "arbitrary"),
        name="sample_attn",
    )(page_table, lamq, lamk, g, qa, kn, vn, *([ck] * npg), *([cv] * npg))


def _gla_kernel(q_ref, k_ref, v_ref, lg_ref, s0_ref, g_ref, o_ref, sout_ref, s_scr, *, chunk):
    t = pl.program_id(1)

    @pl.when(t == 0)
    def _():
        s_scr[...] = s0_ref[...]

    row = lax.broadcasted_iota(I32, (chunk, chunk), 0)
    col = lax.broadcasted_iota(I32, (chunk, chunk), 1)
    causal = col <= row
    tri = causal.astype(F32)
    for c in range(q_ref.shape[0] // chunk):
        rows = slice(c * chunk, (c + 1) * chunk)
        bc = _dot(tri, lg_ref[rows, :], HI)
        bl = bc[chunk - 1:chunk, :]
        k_all = k_ref[rows, :]
        qg_all = (q_ref[rows, :] * jnp.exp(bc)).astype(BF16)
        kg_all = (k_all * jnp.exp(-bc)).astype(BF16)
        kd_all = (k_all * jnp.exp(bl - bc)).astype(BF16)
        decay_all = jnp.transpose(jnp.broadcast_to(jnp.exp(bl), (DV_B, H_B * DK_B)))
        for h in range(H_B):
            dk = slice(h * DK_B, (h + 1) * DK_B)
            dv = slice(h * DV_B, (h + 1) * DV_B)
            qg = qg_all[:, dk]
            v = v_ref[rows, dv].astype(BF16)
            a = jnp.where(causal, lax.dot_general(qg, kg_all[:, dk], _NT, preferred_element_type=F32), 0.0)
            s_prev = s_scr[h]
            o = _dot(qg, s_prev.astype(BF16)) + _dot(a.astype(BF16), v)
            s_scr[h] = decay_all[dk, :] * s_prev + lax.dot_general(kd_all[:, dk], v, _TN,
                                                                   preferred_element_type=F32)
            o_ref[rows, dv] = _rms(o, g_ref[...])

    @pl.when(t == pl.num_programs(1) - 1)
    def _():
        sout_ref[...] = s_scr[...]


def _gla(qb, kb, vb, lg, s0, g, *, batch, seq, chunk):
    tb = min(seq, GLA_TILE)
    assert seq % tb == 0 and tb % chunk == 0
    nt = seq // tb
    tok = lambda b, t: (b * nt + t, 0)
    st = lambda b, t: (b, 0, 0, 0)
    return pl.pallas_call(
        functools.partial(_gla_kernel, chunk=chunk),
        grid=(batch, nt),
        in_specs=[
            pl.BlockSpec((tb, H_B * DK_B), tok),
            pl.BlockSpec((tb, H_B * DK_B), tok),
            pl.BlockSpec((tb, W_B), tok),
            pl.BlockSpec((tb, H_B * DK_B), tok),
            pl.BlockSpec((None, H_B, DK_B, DV_B), st),
            pl.BlockSpec(g.shape, lambda b, t: (0, 0)),
        ],
        out_specs=[
            pl.BlockSpec((tb, W_B), tok),
            pl.BlockSpec((None, H_B, DK_B, DV_B), st),
        ],
        out_shape=[
            jax.ShapeDtypeStruct((batch * seq, W_B), F32),
            jax.ShapeDtypeStruct((batch, H_B, DK_B, DV_B), F32),
        ],
        scratch_shapes=[pltpu.VMEM((H_B, DK_B, DV_B), F32)],
        compiler_params=_cparams("parallel", "arbitrary"),
        name="gla",
    )(qb, kb, vb, lg, s0, g)


def _mix_kernel(x_ref, oa_ref, ob_ref, gb_ref, wo_ref, g1_ref, b1_ref, wr_ref, br_ref,
                x1_ref, gw_ref, idx_ref, cnt_ref, cnt_scr):
    i = pl.program_id(0)

    @pl.when(i == 0)
    def _():
        cnt_scr[...] = jnp.zeros(cnt_scr.shape, F32)

    rc = MIX_ROWS
    lane = lax.broadcasted_iota(I32, (rc, LANES), 1)
    lane_f = lane.astype(F32)
    big = float(LANES)
    neg = -jnp.inf
    r = lax.broadcasted_iota(I32, (rc, rc), 0)
    c = lax.broadcasted_iota(I32, (rc, rc), 1)
    earlier = jnp.where(c < r, 1.0, 0.0).astype(BF16)

    def first_argmax(v, vmax):
        return jnp.min(jnp.where(v == vmax, lane_f, big), axis=1, keepdims=True)

    cnt = cnt_scr[...]
    for ch in range(x_ref.shape[0] // rc):
        rows = slice(ch * rc, (ch + 1) * rc)
        gb = gb_ref[rows, :]
        obg = ob_ref[rows, :] * (gb * jax.nn.sigmoid(gb))
        mix = (_dot(oa_ref[rows, :].astype(BF16), wo_ref[0:W_A, :])
               + _dot(obg.astype(BF16), wo_ref[W_A:D_MODEL, :]))
        x1 = _layernorm(ALPHA * x_ref[rows, :] + mix, g1_ref[...], b1_ref[...])
        x1_ref[rows, :] = x1

        logits = _dot(x1, wr_ref[...], HI) + br_ref[...]
        lg1 = jnp.where((lane >= N_EXPERTS) & (lane < N_EXPERTS + N_GROUPS), logits, neg)
        m1 = jnp.max(lg1, axis=1, keepdims=True)
        pg = 1.0 / jnp.sum(jnp.exp(lg1 - m1), axis=1, keepdims=True)
        grp = first_argmax(lg1, m1) - float(N_EXPERTS)
        in_grp = (lane_f >= grp * EXP_PER_GROUP) & (lane_f < (grp + 1.0) * EXP_PER_GROUP)
        lg2 = jnp.where(in_grp, logits, neg)
        v1 = jnp.max(lg2, axis=1, keepdims=True)
        i1 = first_argmax(lg2, v1)
        lg2b = jnp.where(lane_f == i1, neg, lg2)
        v2 = jnp.max(lg2b, axis=1, keepdims=True)
        i2 = first_argmax(lg2b, v2)
        t = jnp.exp(v2 - v1)
        w1 = pg / (1.0 + t)
        w2 = pg * t / (1.0 + t)

        hit1 = lane_f == i1
        hit2 = lane_f == i2
        sel = jnp.where(hit1 | hit2, 1.0, 0.0)
        rank = _dot(earlier, sel.astype(BF16)) + cnt
        pos1 = jnp.sum(jnp.where(hit1, rank, 0.0), axis=1, keepdims=True)
        pos2 = jnp.sum(jnp.where(hit2, rank, 0.0), axis=1, keepdims=True)
        cnt = cnt + jnp.sum(sel, axis=0, keepdims=True)

        gw_ref[rows, :] = jnp.where(lane == 0, w1, jnp.where(lane == 1, w2, 0.0))
        idx_f = jnp.where(lane == 0, i1, jnp.where(lane == 1, i2, jnp.where(lane == 2, pos1,
                          jnp.where(lane == 3, pos2, 0.0))))
        idx_ref[rows, :] = idx_f.astype(I32)
    cnt_scr[...] = cnt
    cnt_ref[...] = cnt.astype(I32)


def _mix(x, oa, ob, gb, wo, g1, b1, wr, br):
    n = x.shape[0]
    tm = TOKEN_TILE
    row = lambda i: (i, 0)
    fixed = lambda i: (0, 0)
    return pl.pallas_call(
        _mix_kernel,
        grid=(n // tm,),
        in_specs=[
            pl.BlockSpec((tm, D_MODEL), row),
            pl.BlockSpec((tm, W_A), row),
            pl.BlockSpec((tm, W_B), row),
            pl.BlockSpec((tm, W_B), row),
            pl.BlockSpec(wo.shape, fixed),
            pl.BlockSpec(g1.shape, fixed),
            pl.BlockSpec(b1.shape, fixed),
            pl.BlockSpec(wr.shape, fixed),
            pl.BlockSpec(br.shape, fixed),
        ],
        out_specs=[
            pl.BlockSpec((tm, D_MODEL), row),
            pl.BlockSpec((tm, LANES), row),
            pl.BlockSpec((tm, LANES), row),
            pl.BlockSpec((1, LANES), fixed),
        ],
        out_shape=[
            jax.ShapeDtypeStruct((n, D_MODEL), F32),
            jax.ShapeDtypeStruct((n, LANES), F32),
            jax.ShapeDtypeStruct((n, LANES), I32),
            jax.ShapeDtypeStruct((1, LANES), I32),
        ],
        scratch_shapes=[pltpu.VMEM((1, LANES), F32)],
        compiler_params=_cparams("arbitrary"),
        name="mix_route",
    )(x, oa, ob, gb, wo, g1, b1, wr, br)


def _row_copy(src, dst, sem):
    return pltpu.make_async_copy(src, dst, sem)


def _dispatch_kernel(dst_ref, x_ref, buf_in_ref, buf_ref, sem):
    del buf_in_ref
    tm = x_ref.shape[0]
    for t in range(tm):
        for k in range(2):
            dst = dst_ref[0, 0, 2 * t + k]
            _row_copy(x_ref.at[pl.ds(t, 1), :], buf_ref.at[pl.ds(dst, 1), :], sem).start()
    for _ in range(2 * tm):
        _row_copy(x_ref.at[pl.ds(0, 1), :], buf_ref.at[pl.ds(0, 1), :], sem).wait()


def _dispatch(dst, x1, n_rows):
    n = x1.shape[0]
    tm = TOKEN_TILE
    return pl.pallas_call(
        _dispatch_kernel,
        grid=(n // tm,),
        in_specs=[
            pl.BlockSpec((1, 1, 2 * tm), lambda i: (i, 0, 0), memory_space=pltpu.SMEM),
            pl.BlockSpec((tm, D_MODEL), lambda i: (i, 0)),
            pl.BlockSpec(memory_space=pl.ANY),
        ],
        out_specs=pl.BlockSpec(memory_space=pl.ANY),
        scratch_shapes=[pltpu.SemaphoreType.DMA(())],
        out_shape=jax.ShapeDtypeStruct((n_rows, D_MODEL), F32),
        input_output_aliases={2: 0},
        compiler_params=_cparams("arbitrary"),
        name="dispatch",
    )(dst, x1, jnp.zeros((n_rows, D_MODEL), F32))


def _expert_kernel(be_ref, nu_ref, x_ref, wg_ref, wu_ref, wd_ref, y_ref, wgb, wub, wdb):
    j = pl.program_id(0)

    @pl.when(j < nu_ref[0])
    def _():
        prev = be_ref[jnp.maximum(j - 1, 0)]

        @pl.when((j == 0) | (be_ref[j] != prev))
        def _():
            wgb[...] = wg_ref[...].astype(BF16)
            wub[...] = wu_ref[...].astype(BF16)
            wdb[...] = wd_ref[...].astype(BF16)

        x = x_ref[...].astype(BF16)
        hg = _dot(x, wgb[...])
        hu = _dot(x, wub[...])
        h = hg * jax.nn.sigmoid(hg) * hu
        y_ref[...] = _dot(h.astype(BF16), wdb[...])

    @pl.when(j >= nu_ref[0])
    def _():
        y_ref[...] = jnp.zeros(y_ref.shape, F32)


def _experts(block_e, n_used, xbuf, w_gate, w_up, w_down, *, li):
    n_rows = xbuf.shape[0]
    blk = EXPERT_BLOCK
    nb = n_rows // blk

    def rows(j, be, nu):
        return (jnp.minimum(j, nu[0] - 1), 0)

    def wsel(j, be, nu):
        return (li, be[jnp.minimum(j, nu[0] - 1)], 0, 0)

    grid_spec = pltpu.PrefetchScalarGridSpec(
        num_scalar_prefetch=2,
        grid=(nb,),
        in_specs=[
            pl.BlockSpec((blk, D_MODEL), rows),
            pl.BlockSpec((None, None, D_MODEL, D_EXPERT), wsel),
            pl.BlockSpec((None, None, D_MODEL, D_EXPERT), wsel),
            pl.BlockSpec((None, None, D_EXPERT, D_MODEL), wsel),
        ],
        out_specs=pl.BlockSpec((blk, D_MODEL), lambda j, be, nu: (j, 0)),
        scratch_shapes=[
            pltpu.VMEM((D_MODEL, D_EXPERT), BF16),
            pltpu.VMEM((D_MODEL, D_EXPERT), BF16),
            pltpu.VMEM((D_EXPERT, D_MODEL), BF16),
        ],
    )
    return pl.pallas_call(
        _expert_kernel,
        grid_spec=grid_spec,
        out_shape=jax.ShapeDtypeStruct((n_rows, D_MODEL), F32),
        compiler_params=_cparams("arbitrary"),
        name="experts",
    )(block_e, n_used, xbuf, w_gate, w_up, w_down)


def _combine_kernel(cur_ref, nxt_ref, gw_ref, x1_ref, p_ref, ybuf_ref, wpp_ref, wpg_ref,
                    g2_ref, b2_ref, gp_ref, o_ref, y_scr, sems):
    tm = x1_ref.shape[0]
    i = pl.program_id(0)
    slot = i & 1

    def gather(src_ref, s):
        for t in range(tm):
            for k in range(2):
                src = src_ref[0, 0, 2 * t + k]
                _row_copy(ybuf_ref.at[pl.ds(src, 1), :], y_scr.at[s, k, pl.ds(t, 1), :], sems.at[s]).start()

    @pl.when(i == 0)
    def _():
        gather(cur_ref, 0)

    @pl.when(i + 1 < pl.num_programs(0))
    def _():
        gather(nxt_ref, 1 - slot)

    for _ in range(2 * tm):
        _row_copy(ybuf_ref.at[pl.ds(0, 1), :], y_scr.at[slot, 0, pl.ds(0, 1), :], sems.at[slot]).wait()

    gw = gw_ref[...]
    y = gw[:, 0:1] * y_scr[slot, 0] + gw[:, 1:2] * y_scr[slot, 1]
    x2 = _layernorm(ALPHA * x1_ref[...] + y, g2_ref[...], b2_ref[...])
    e = _rms(_dot(p_ref[...].astype(BF16), wpp_ref[...]), gp_ref[...])
    o_ref[...] = x2 + jax.nn.sigmoid(_dot(x2.astype(BF16), wpg_ref[...])) * e


def _combine(dst, gw, x1, p_l, ybuf, wpp, wpg, g2, b2, gp):
    n = x1.shape[0]
    tm = TOKEN_TILE
    nt = n // tm
    row = lambda i: (i, 0)
    fixed = lambda i: (0, 0)
    return pl.pallas_call(
        _combine_kernel,
        grid=(nt,),
        in_specs=[
            pl.BlockSpec((1, 1, 2 * tm), lambda i: (i, 0, 0), memory_space=pltpu.SMEM),
            pl.BlockSpec((1, 1, 2 * tm), lambda i: (jnp.minimum(i + 1, nt - 1), 0, 0), memory_space=pltpu.SMEM),
            pl.BlockSpec((tm, LANES), row),
            pl.BlockSpec((tm, D_MODEL), row),
            pl.BlockSpec((tm, D_PLE), row),
            pl.BlockSpec(memory_space=pl.ANY),
            pl.BlockSpec(wpp.shape, fixed),
            pl.BlockSpec(wpg.shape, fixed),
            pl.BlockSpec(g2.shape, fixed),
            pl.BlockSpec(b2.shape, fixed),
            pl.BlockSpec(gp.shape, fixed),
        ],
        out_specs=pl.BlockSpec((tm, D_MODEL), row),
        scratch_shapes=[pltpu.VMEM((2, 2, tm, D_MODEL), F32), pltpu.SemaphoreType.DMA((2,))],
        out_shape=jax.ShapeDtypeStruct((n, D_MODEL), F32),
        compiler_params=_cparams("arbitrary"),
        name="combine",
    )(dst, dst, gw, x1, p_l, ybuf, wpp, wpg, g2, b2, gp)


def _rope_tables(pos):
    half = HD_A // 2
    inv = ROPE_THETA ** (-jnp.arange(half, dtype=F32) / half)
    ang = pos.astype(F32)[:, None] * inv[None, :]
    cos, sin = jnp.cos(ang), jnp.sin(ang)
    reps = LANES // HD_A
    return (jnp.tile(jnp.concatenate([cos, cos], axis=1), (1, reps)),
            jnp.tile(jnp.concatenate([-sin, sin], axis=1), (1, reps)))


def _moe_plan(counts, n_tokens):
    blk = EXPERT_BLOCK
    nb = (2 * n_tokens) // blk + N_EXPERTS
    padded = (counts + blk - 1) // blk * blk
    e = jnp.arange(N_EXPERTS, dtype=I32)
    pends = jnp.sum(jnp.where(e[None, :] <= e[:, None], padded[None, :], 0), axis=1)
    pstart = (pends - padded).astype(I32)
    first_row = jnp.arange(nb, dtype=I32) * blk
    block_e = jnp.minimum(jnp.sum(pends[None, :] <= first_row[:, None], axis=1), N_EXPERTS - 1).astype(I32)
    n_used = (pends[-1:] // blk).astype(I32)
    return pstart, block_e, n_used, nb * blk


def _layer(x, p_l, li, lw, rope, attend, s0, *, batch, seq, chunk, groups, kv_prev):
    cos_t, sin_t = rope
    qa, ka, qb, kb, vb, gb, lg, vt, k4, v4 = _inproj(
        x, lw["w_main"], lw["w_vt"], lw["w_glr"], lw["w_gk2"], lw["b_gk2"], cos_t, sin_t,
        groups=groups, kv_prev=kv_prev)
    oa = attend(qa, ka, vt, k4, v4)
    ob, s_out = _gla(qb, kb, vb, lg, s0, lw["gla_g"], batch=batch, seq=seq, chunk=chunk)
    x1, gw, idx, cnt = _mix(x, oa, ob, gb, lw["w_o"], lw["ln1_g"], lw["ln1_b"], lw["w_r"], lw["b_r"])
    n = x.shape[0]
    pstart, block_e, n_used, n_rows = _moe_plan(cnt[0, :N_EXPERTS], n)
    eid, pos = idx[:, 0:2], idx[:, 2:4]
    first = jnp.sum(jnp.where(eid[:, :, None] == jnp.arange(N_EXPERTS, dtype=I32), pstart, 0), axis=-1)
    dst = (first + pos).reshape(n // TOKEN_TILE, 1, 2 * TOKEN_TILE)
    xbuf = _dispatch(dst, x1, n_rows)
    ybuf = _experts(block_e, n_used, xbuf, lw["w_gate"], lw["w_up"], lw["w_down"], li=li)
    out = _combine(dst, gw, x1, p_l, ybuf, lw["w_pp"], lw["w_pg"], lw["ln2_g"], lw["ln2_b"], lw["ple_g"])
    return out, k4, v4, s_out


def kernel(x_prompt, x_sample, cache_k, cache_v, state_gla, page_table, p_prompt, p_sample, w_in, w_gk2, b_gk2, lam_q1, lam_k1, lam_q2, lam_k2, diff_norm_g, gla_norm_g, w_o, ln1_g, ln1_b, w_r1, b_r1, w_r2, b_r2, w_gate, w_up, w_down, ln2_g, ln2_b, w_ple_gate, w_ple_proj, ple_norm_g):
    bp, tp, _ = x_prompt.shape
    bs, ts, _ = x_sample.shape
    past = page_table.shape[1] * PAGE_SIZE
    rope_p = _rope_tables(jnp.arange(tp))
    rope_s = tuple(jnp.tile(t, (TOKEN_TILE // ts, 1)) for t in _rope_tables(past + jnp.arange(ts)))

    row2 = lambda a: a.reshape(1, -1)
    yp = x_prompt.reshape(bp * tp, D_MODEL)
    ys = x_sample.reshape(bs * ts, D_MODEL)
    s0_p = jnp.zeros((bp, H_B, DK_B, DV_B), F32)
    outs = {k: [] for k in ("sp", "ks", "vs", "ss")}
    kv_p = [None, None]
    for li in range(DEPTH):
        lw = {
            "w_main": w_in[li, :, :_OFF_GLR].astype(BF16),
            "w_vt": jnp.transpose(w_in[li, :, _OFF_VA:_OFF_VA + W_A]).astype(BF16),
            "w_glr": jnp.pad(w_in[li, :, _OFF_GLR:], ((0, 0), (0, LANES - GATE_RANK))).astype(BF16),
            "w_gk2": jnp.pad(w_gk2[li], ((0, LANES - GATE_RANK), (0, 0))),
            "b_gk2": row2(b_gk2[li]),
            "gla_g": row2(gla_norm_g[li]),
            "w_o": w_o[li].astype(BF16),
            "ln1_g": row2(ln1_g[li]), "ln1_b": row2(ln1_b[li]),
            "w_r": jnp.pad(jnp.concatenate([w_r2[li], w_r1[li]], axis=1),
                           ((0, 0), (0, LANES - N_EXPERTS - N_GROUPS))),
            "b_r": row2(jnp.pad(jnp.concatenate([b_r2[li], b_r1[li]]), (0, LANES - N_EXPERTS - N_GROUPS))),
            "w_gate": w_gate, "w_up": w_up, "w_down": w_down,
            "w_pp": w_ple_proj[li].astype(BF16),
            "w_pg": w_ple_gate[li].astype(BF16),
            "ln2_g": row2(ln2_g[li]), "ln2_b": row2(ln2_b[li]),
            "ple_g": row2(ple_norm_g[li]),
        }
        lamq = jnp.stack([lam_q1[li], lam_q2[li]])
        lamk = jnp.stack([lam_k1[li], lam_k2[li]])
        dg = row2(diff_norm_g[li])

        def attend_p(qa, ka, vt, k4, v4, lamq=lamq, lamk=lamk, dg=dg, li=li):
            del k4, v4
            return _prompt_attention(qa, ka, vt, lamq, lamk, dg, li=li, batch=bp, seq=tp)

        yp, kv_p[0], kv_p[1], s = _layer(
            yp, p_prompt[li].reshape(bp * tp, D_PLE), li, lw, rope_p, attend_p, s0_p,
            batch=bp, seq=tp, chunk=GLA_CHUNK, groups=bp, kv_prev=None if li == 0 else tuple(kv_p))
        outs["sp"].append(s)

        def attend_s(qa, ka, vt, k4, v4, lamq=lamq, lamk=lamk, dg=dg, li=li):
            del ka, vt
            return _sample_attention(qa, k4.reshape(-1, DKV_A), v4.reshape(-1, DKV_A), cache_k, cache_v,
                                     page_table, lamq, lamk, dg, li=li, batch=bs, ts=ts)

        ys, k4, v4, s = _layer(ys, p_sample[li].reshape(bs * ts, D_PLE), li, lw, rope_s, attend_s,
                               state_gla[:, li], batch=bs, seq=ts, chunk=ts, groups=1, kv_prev=None)
        outs["ks"].append(k4.reshape(bs, ts, H_A, DKV_A))
        outs["vs"].append(v4.reshape(bs, ts, H_A, DKV_A))
        outs["ss"].append(s)

    stack = lambda k: jnp.stack(outs[k], axis=1)
    return (yp.reshape(bp, tp, D_MODEL), ys.reshape(bs, ts, D_MODEL),
            kv_p[0].reshape(bp, DEPTH, tp, H_A, DKV_A), kv_p[1].reshape(bp, DEPTH, tp, H_A, DKV_A),
            stack("sp"), stack("ks"), stack("vs"), stack("ss"))
```

```python
import functools
import math

import jax
import jax.numpy as jnp
from jax import lax
from jax.experimental import pallas as pl
from jax.experimental.pallas import tpu as pltpu

F32 = jnp.float32
BF16 = jnp.bfloat16
I32 = jnp.int32
HI = lax.Precision.HIGHEST

D_MODEL = 1024
DEPTH = 2
PAGE_SIZE = 128
D_PLE = 256
HD_A = 64
DKV_A = 2 * HD_A
W_A = D_MODEL // 2
H_A = W_A // DKV_A
W_B = D_MODEL - W_A
H_B = 4
DV_B = W_B // H_B
DK_B = DV_B // 2
GATE_RANK = 16
GATE_NORM = 16.0
GLA_CHUNK = 64
N_GROUPS = 4
EXP_PER_GROUP = 8
N_EXPERTS = N_GROUPS * EXP_PER_GROUP
D_EXPERT = D_MODEL // 2
ROPE_THETA = 10000.0
ALPHA = (2 * DEPTH) ** 0.25
EPS = 1e-5
LOG2E = math.log2(math.e)

LANES = 128
VMEM_LIMIT = 48 * 1024 * 1024

TOKEN_TILE = 256
ATTN_TILE = 256
PAGES_PER_STEP = 8
PAGE_SLOTS = 3
PAGE_GROUP = 4
GLA_TILE = 256
EXPERT_BLOCK = 256
MIX_ROWS = 128

_OFF_QA, _OFF_KA, _OFF_VA = 0, W_A, 2 * W_A
_OFF_QB = 3 * W_A
_OFF_KB = _OFF_QB + H_B * DK_B
_OFF_VB = _OFF_KB + H_B * DK_B
_OFF_GB = _OFF_VB + W_B
_OFF_GLR = _OFF_GB + W_B

_NT = (((1,), (1,)), ((), ()))
_TN = (((0,), (0,)), ((), ()))


def _cparams(*sem):
    return pltpu.CompilerParams(dimension_semantics=sem, vmem_limit_bytes=VMEM_LIMIT)


def _dot(a, b, precision=None):
    return jnp.dot(a, b, preferred_element_type=F32, precision=precision)


def _lam_init(li):
    return 0.8 - 0.6 * math.exp(-0.3 * li)


def _lam(lamq_ref, lamk_ref, li):
    s = jnp.sum(lamq_ref[...] * lamk_ref[...], axis=1, keepdims=True)
    e = jnp.exp(s)
    return e[0:1, :] - e[1:2, :] + _lam_init(li)


def _rms(x, g):
    return x * lax.rsqrt(jnp.mean(x * x, axis=-1, keepdims=True) + EPS) * g


def _layernorm(x, g, b):
    mu = jnp.mean(x, axis=-1, keepdims=True)
    xc = x - mu
    var = jnp.mean(xc * xc, axis=-1, keepdims=True)
    return xc * lax.rsqrt(var + EPS) * g + b


def _inproj_kernel(x_ref, w_ref, wvt_ref, wglr_ref, wgk2_ref, bgk2_ref, cos_ref, sin_ref, *refs):
    qa_ref, ka_ref, qb_ref, kb_ref, vb_ref, gb_ref, lg_ref, vt_ref, k4_ref, v4_ref = refs[-10:]
    tm = x_ref.shape[0]
    layer = k4_ref.shape[0] - 1
    if len(refs) > 10:
        k4_ref[0:layer] = refs[0][...]
        v4_ref[0:layer] = refs[1][...]
    xb = x_ref[...].astype(BF16)
    vt_ref[...] = lax.dot_general(wvt_ref[...], xb, _NT, preferred_element_type=F32).astype(BF16)
    cos = cos_ref[...]
    sin = sin_ref[...]
    lane = lax.broadcasted_iota(I32, cos.shape, 1)
    first_half = (lane & (HD_A // 2)) == 0

    def rope(h):
        partner = jnp.where(first_half, pltpu.roll(h, LANES - HD_A // 2, 1), pltpu.roll(h, HD_A // 2, 1))
        return h * cos + partner * sin

    for c in range(W_A // LANES):
        lo = c * LANES
        hq = _dot(xb, w_ref[:, _OFF_QA + lo:_OFF_QA + lo + LANES])
        qa_ref[:, lo:lo + LANES] = rope(hq) * (HD_A ** -0.5 * LOG2E)
        hk = rope(_dot(xb, w_ref[:, _OFF_KA + lo:_OFF_KA + lo + LANES]))
        ka_ref[:, lo:lo + LANES] = hk
        k4_ref[layer, pl.ds(c, tm, stride=H_A), :] = hk
        v4_ref[layer, pl.ds(c, tm, stride=H_A), :] = _dot(xb, w_ref[:, _OFF_VA + lo:_OFF_VA + lo + LANES])
    qb_ref[...] = _dot(xb, w_ref[:, _OFF_QB:_OFF_KB]) * (DK_B ** -0.5)
    kb_ref[...] = _dot(xb, w_ref[:, _OFF_KB:_OFF_VB])
    vb_ref[...] = _dot(xb, w_ref[:, _OFF_VB:_OFF_GB])
    gb_ref[...] = _dot(xb, w_ref[:, _OFF_GB:_OFF_GLR])
    glr = _dot(xb, wglr_ref[...])
    z = _dot(glr, wgk2_ref[...], HI) + bgk2_ref[...]
    lg_ref[...] = (jnp.minimum(z, 0.0) - jnp.log1p(jnp.exp(-jnp.abs(z)))) * (1.0 / GATE_NORM)


def _inproj(x, w_main, w_vt, w_glr, w_gk2p, b_gk2, cos_t, sin_t, *, groups, kv_prev=None):
    n = x.shape[0]
    tm = TOKEN_TILE
    ntab = cos_t.shape[0] // tm
    nq = n // tm // groups
    kv_idx = lambda i: (i // nq, 0, i % nq, 0)
    layers = 1 if kv_prev is None else kv_prev[0].shape[1] + 1
    row = lambda i: (i, 0)
    fixed = lambda i: (0, 0)
    tab = lambda i: (i % ntab, 0)
    widths = (W_A, W_A, H_B * DK_B, H_B * DK_B, W_B, W_B, H_B * DK_B)
    operands = [x, w_main, w_vt, w_glr, w_gk2p, b_gk2, cos_t, sin_t]
    in_specs = [
        pl.BlockSpec((tm, D_MODEL), row),
        pl.BlockSpec(w_main.shape, fixed),
        pl.BlockSpec(w_vt.shape, fixed),
        pl.BlockSpec(w_glr.shape, fixed),
        pl.BlockSpec(w_gk2p.shape, fixed),
        pl.BlockSpec(b_gk2.shape, fixed),
        pl.BlockSpec((tm, LANES), tab),
        pl.BlockSpec((tm, LANES), tab),
    ]
    if kv_prev is not None:
        operands += list(kv_prev)
        in_specs += [pl.BlockSpec((None, layers - 1, tm * H_A, DKV_A), kv_idx)] * 2
    kv_shape = jax.ShapeDtypeStruct((groups, layers, nq * tm * H_A, DKV_A), F32)
    return pl.pallas_call(
        _inproj_kernel,
        grid=(n // tm,),
        in_specs=in_specs,
        out_specs=[pl.BlockSpec((tm, w), row) for w in widths]
        + [pl.BlockSpec((None, W_A, tm), lambda i: (i, 0, 0))]
        + [pl.BlockSpec((None, layers, tm * H_A, DKV_A), kv_idx)] * 2,
        out_shape=[jax.ShapeDtypeStruct((n, w), F32) for w in widths]
        + [jax.ShapeDtypeStruct((n // tm, W_A, tm), BF16), kv_shape, kv_shape],
        compiler_params=_cparams("parallel"),
        name="inproj",
    )(*operands)


def _pattn_kernel(lamq_ref, lamk_ref, g_ref, q_ref, k_ref, vt_ref, o_ref, acc_scr, sa_scr, sb_scr, *, li):
    tq = q_ref.shape[0]
    win = 2 * tq
    last_win = k_ref.shape[0] // win - 1
    qi = pl.program_id(2)
    n_pairs = lax.shift_right_logical(qi, 2)
    q = q_ref[...]
    lane = lax.broadcasted_iota(I32, q.shape, 1)
    qc = (jnp.where(lane < HD_A, q, 0.0).astype(BF16), jnp.where(lane >= HD_A, q, 0.0).astype(BF16))
    acc_scr[...] = jnp.zeros(acc_scr.shape, F32)

    def scores(w, s_scr):
        w = jnp.minimum(w, last_win)
        start = pl.multiple_of(w * win, win)
        kb = k_ref[pl.ds(start, win), :].astype(BF16)
        for c in range(2):
            s_scr[c] = lax.dot_general(kb, qc[c], _NT, preferred_element_type=F32)

    def consume(w, s_scr, stats, masked):
        vt0 = vt_ref[2 * w]
        vt1 = vt_ref[2 * w + 1]
        out = []
        for c in range(2):
            m_prev, l_prev = stats[c]
            st = s_scr[c]
            if masked:
                key = lax.broadcasted_iota(I32, st.shape, 0)
                qry = lax.broadcasted_iota(I32, (1, tq), 1) + (qi * tq - w * win)
                st = jnp.where(key <= qry, st, -jnp.inf)
            m_new = jnp.maximum(m_prev, jnp.max(st, axis=0, keepdims=True))
            alpha = jnp.exp2(m_prev - m_new)
            p = jnp.exp2(st - m_new)
            l_new = alpha * l_prev + jnp.sum(p, axis=0, keepdims=True)
            p = p.astype(BF16)
            cols = slice(c * tq, (c + 1) * tq)
            acc_scr[:, cols] = alpha * acc_scr[:, cols] + (_dot(vt0, p[0:tq]) + _dot(vt1, p[tq:win]))
            out.append((m_new, l_new))
        return tuple(out)

    def pair(i, stats):
        scores(2 * i + 1, sb_scr)
        stats = consume(2 * i, sa_scr, stats, False)
        scores(2 * i + 2, sa_scr)
        return consume(2 * i + 1, sb_scr, stats, False)

    def last_pair(i, stats):
        def both(s):
            scores(2 * i + 1, sb_scr)
            s = consume(2 * i, sa_scr, s, True)
            return consume(2 * i + 1, sb_scr, s, True)

        def first_only(s):
            return consume(2 * i, sa_scr, s, True)

        reaches_second = (qi & 3) >= 2
        return lax.cond(reaches_second, both, first_only, stats)

    scores(0, sa_scr)
    stat = (jnp.full((1, tq), -jnp.inf, F32), jnp.zeros((1, tq), F32))
    stats = lax.fori_loop(0, n_pairs, pair, (stat, stat))
    (_, l1), (_, l2) = last_pair(n_pairs, stats)
    out_t = acc_scr[:, 0:tq] / l1 - _lam(lamq_ref, lamk_ref, li) * (acc_scr[:, tq:2 * tq] / l2)
    o_ref[...] = _rms(jnp.transpose(out_t), g_ref[...]) * (1.0 - _lam_init(li))


def _prompt_attention(qa, ka, vt, lamq, lamk, g, *, li, batch, seq):
    tq = ATTN_TILE
    assert tq == TOKEN_TILE and seq % (4 * tq) == 0
    nq = seq // tq
    small = lambda b, h, i: (0, 0)
    return pl.pallas_call(
        functools.partial(_pattn_kernel, li=li),
        grid=(batch, H_A, nq),
        in_specs=[
            pl.BlockSpec(lamq.shape, small),
            pl.BlockSpec(lamk.shape, small),
            pl.BlockSpec(g.shape, small),
            pl.BlockSpec((tq, DKV_A), lambda b, h, i: (b * nq + i, h)),
            pl.BlockSpec((seq, DKV_A), lambda b, h, i: (b, h)),
            pl.BlockSpec((nq, DKV_A, tq), lambda b, h, i: (b, h, 0)),
        ],
        out_specs=pl.BlockSpec((tq, DKV_A), lambda b, h, i: (b * nq + i, h)),
        out_shape=jax.ShapeDtypeStruct(qa.shape, F32),
        scratch_shapes=[pltpu.VMEM((DKV_A, 2 * tq), F32),
                        pltpu.VMEM((2, 2 * tq, tq), F32),
                        pltpu.VMEM((2, 2 * tq, tq), F32)],
        compiler_params=_cparams("parallel", "parallel", "arbitrary"),
        name="prompt_attn",
    )(lamq, lamk, g, qa, ka, vt)


def _sattn_kernel(pt_ref, lamq_ref, lamk_ref, g_ref, q_ref, kn_ref, vn_ref, ck_ref, cv_ref, o_ref,
                  qt_scr, m_scr, l_scr, acc_scr, kn_scr, vn_scr, kbuf, vbuf, sems, *, li, npages, ts):
    step = pl.program_id(1)
    steps_per_seq = pl.num_programs(1)
    n_steps = pl.num_programs(0) * steps_per_seq
    lin = pl.program_id(0) * steps_per_seq + step
    slot = lin % PAGE_SLOTS
    nrow = H_A * 2 * ts
    rows_per_page = PAGE_SIZE * H_A

    def page_copies(t, s):
        b = t // steps_per_seq
        g = t % steps_per_seq
        out = []
        for i in range(npages):
            page = pt_ref[b, g * npages + i]
            out.append(pltpu.make_async_copy(ck_ref.at[page, li], kbuf.at[s, i], sems.at[s]))
            out.append(pltpu.make_async_copy(cv_ref.at[page, li], vbuf.at[s, i], sems.at[s]))
        return out

    @pl.when(lin == 0)
    def _():
        for t in range(PAGE_SLOTS - 1):
            for cp in page_copies(t, t):
                cp.start()

    ahead = lin + (PAGE_SLOTS - 1)

    @pl.when(ahead < n_steps)
    def _():
        for cp in page_copies(ahead, ahead % PAGE_SLOTS):
            cp.start()

    def head_match(shape):
        row = lax.broadcasted_iota(I32, shape, 0)
        col = lax.broadcasted_iota(I32, shape, 1)
        return (col & (H_A - 1)) == (row // (2 * ts)), row, col

    def update(state, scores, values):
        m_prev, l_prev, acc = state
        m_new = m_prev
        for s in scores:
            m_new = jnp.maximum(m_new, jnp.max(s, axis=1, keepdims=True))
        alpha = jnp.exp2(m_prev - m_new)
        l_new = alpha * l_prev
        acc = alpha * acc
        for s, v_bf in zip(scores, values):
            p = jnp.exp2(s - m_new)
            l_new = l_new + jnp.sum(p, axis=1, keepdims=True)
            acc = acc + _dot(p.astype(BF16), v_bf)
        return m_new, l_new, acc

    @pl.when(step == 0)
    def _():
        q = q_ref[...]
        lane = lax.broadcasted_iota(I32, (ts, DKV_A), 1)
        blocks = []
        for h in range(H_A):
            qh = q[:, h * DKV_A:(h + 1) * DKV_A]
            blocks += [jnp.where(lane < HD_A, qh, 0.0), jnp.where(lane >= HD_A, qh, 0.0)]
        qt = jnp.concatenate(blocks, axis=0).astype(BF16)
        qt_scr[...] = qt
        kn_scr[...] = jnp.zeros(kn_scr.shape, F32)
        vn_scr[...] = jnp.zeros(vn_scr.shape, F32)
        kn_scr[0:ts * H_A, :] = kn_ref[...]
        vn_scr[0:ts * H_A, :] = vn_ref[...]
        s = lax.dot_general(qt, kn_scr[...].astype(BF16), _NT, preferred_element_type=F32)
        match, row, col = head_match(s.shape)
        ok = match & ((col // H_A) <= (row & (ts - 1))) & (col < ts * H_A)
        s = jnp.where(ok, s, -jnp.inf)
        init = (jnp.full((nrow, 1), -jnp.inf, F32), jnp.zeros((nrow, 1), F32), jnp.zeros((nrow, DKV_A), F32))
        m, l, acc = update(init, [s], [vn_scr[...].astype(BF16)])
        m_scr[...] = jnp.broadcast_to(m, m_scr.shape)
        l_scr[...] = jnp.broadcast_to(l, l_scr.shape)
        acc_scr[...] = acc

    for cp in page_copies(lin, slot):
        cp.wait()

    qt = qt_scr[...]
    match, _, _ = head_match((nrow, rows_per_page))
    scores = [jnp.where(match, lax.dot_general(qt, kbuf[slot, i].astype(BF16), _NT, preferred_element_type=F32),
                        -jnp.inf) for i in range(npages)]
    state = (m_scr[:, 0:1], l_scr[:, 0:1], acc_scr[...])
    for g0 in range(0, npages, PAGE_GROUP):
        state = update(state, scores[g0:g0 + PAGE_GROUP],
                       [vbuf[slot, i].astype(BF16) for i in range(g0, g0 + PAGE_GROUP)])
    m, l, acc = state
    m_scr[...] = jnp.broadcast_to(m, m_scr.shape)
    l_scr[...] = jnp.broadcast_to(l, l_scr.shape)
    acc_scr[...] = acc

    @pl.when(step == steps_per_seq - 1)
    def _():
        o = acc_scr[...] / l_scr[:, 0:1]
        lam = _lam(lamq_ref, lamk_ref, li)
        for h in range(H_A):
            r0 = h * 2 * ts
            out = o[r0:r0 + ts, :] - lam * o[r0 + ts:r0 + 2 * ts, :]
            o_ref[:, h * DKV_A:(h + 1) * DKV_A] = _rms(out, g_ref[...]) * (1.0 - _lam_init(li))


def _sample_attention(qa, kn, vn, cache_k, cache_v, page_table, lamq, lamk, g, *, li, batch, ts):
    npg = PAGES_PER_STEP
    n_pages = page_table.shape[1]
    assert n_pages % npg == 0 and npg % PAGE_GROUP == 0 and ts == 8 and ts * H_A <= PAGE_SIZE
    assert batch * (n_pages // npg) >= PAGE_SLOTS
    nrow = H_A * 2 * ts
    rows_per_page = PAGE_SIZE * H_A
    small = lambda b, s, pt: (0, 0)
    new = lambda b, s, pt: (b, 0)
    grid_spec = pltpu.PrefetchScalarGridSpec(
        num_scalar_prefetch=1,
        grid=(batch, n_pages // npg),
        in_specs=[
            pl.BlockSpec(lamq.shape, small),
            pl.BlockSpec(lamk.shape, small),
            pl.BlockSpec(g.shape, small),
            pl.BlockSpec((ts, W_A), new),
            pl.BlockSpec((ts * H_A, DKV_A), new),
            pl.BlockSpec((ts * H_A, DKV_A), new),
            pl.BlockSpec(memory_space=pl.ANY),
            pl.BlockSpec(memory_space=pl.ANY),
        ],
        out_specs=pl.BlockSpec((ts, W_A), new),
        scratch_shapes=[
            pltpu.VMEM((nrow, DKV_A), BF16),
            pltpu.VMEM((nrow, LANES), F32),
            pltpu.VMEM((nrow, LANES), F32),
            pltpu.VMEM((nrow, DKV_A), F32),
            pltpu.VMEM((PAGE_SIZE, DKV_A), F32),
            pltpu.VMEM((PAGE_SIZE, DKV_A), F32),
            pltpu.VMEM((PAGE_SLOTS, npg, rows_per_page, DKV_A), F32),
            pltpu.VMEM((PAGE_SLOTS, npg, rows_per_page, DKV_A), F32),
            pltpu.SemaphoreType.DMA((PAGE_SLOTS,)),
        ],
    )
    ck = cache_k.reshape(cache_k.shape[0], DEPTH, rows_per_page, DKV_A)
    cv = cache_v.reshape(cache_v.shape[0], DEPTH, rows_per_page, DKV_A)
    return pl.pallas_call(
        functools.partial(_sattn_kernel, li=li, npages=npg, ts=ts),
        grid_spec=grid_spec,
        out_shape=jax.ShapeDtypeStruct(qa.shape, F32),
        compiler_params=_cparams("arbitrary", "arbitrary"),
        name="sample_attn",
    )(page_table, lamq, lamk, g, qa, kn, vn, ck, cv)


def _gla_kernel(q_ref, k_ref, v_ref, lg_ref, s0_ref, g_ref, o_ref, sout_ref, s_scr, *, chunk):
    t = pl.program_id(1)

    @pl.when(t == 0)
    def _():
        s_scr[...] = s0_ref[...]

    row = lax.broadcasted_iota(I32, (chunk, chunk), 0)
    col = lax.broadcasted_iota(I32, (chunk, chunk), 1)
    causal = col <= row
    tri = causal.astype(F32)
    for c in range(q_ref.shape[0] // chunk):
        rows = slice(c * chunk, (c + 1) * chunk)
        bc = _dot(tri, lg_ref[rows, :], HI)
        bl = bc[chunk - 1:chunk, :]
        k_all = k_ref[rows, :]
        qg_all = (q_ref[rows, :] * jnp.exp(bc)).astype(BF16)
        kg_all = (k_all * jnp.exp(-bc)).astype(BF16)
        kd_all = (k_all * jnp.exp(bl - bc)).astype(BF16)
        decay_all = jnp.transpose(jnp.broadcast_to(jnp.exp(bl), (DV_B, H_B * DK_B)))
        for h in range(H_B):
            dk = slice(h * DK_B, (h + 1) * DK_B)
            dv = slice(h * DV_B, (h + 1) * DV_B)
            qg = qg_all[:, dk]
            v = v_ref[rows, dv].astype(BF16)
            a = jnp.where(causal, lax.dot_general(qg, kg_all[:, dk], _NT, preferred_element_type=F32), 0.0)
            s_prev = s_scr[h]
            o = _dot(qg, s_prev.astype(BF16)) + _dot(a.astype(BF16), v)
            s_scr[h] = decay_all[dk, :] * s_prev + lax.dot_general(kd_all[:, dk], v, _TN,
                                                                   preferred_element_type=F32)
            o_ref[rows, dv] = _rms(o, g_ref[...])

    @pl.when(t == pl.num_programs(1) - 1)
    def _():
        sout_ref[...] = s_scr[...]


def _gla(qb, kb, vb, lg, s0, g, *, batch, seq, chunk):
    tb = min(seq, GLA_TILE)
    assert seq % tb == 0 and tb % chunk == 0
    nt = seq // tb
    tok = lambda b, t: (b * nt + t, 0)
    st = lambda b, t: (b, 0, 0, 0)
    return pl.pallas_call(
        functools.partial(_gla_kernel, chunk=chunk),
        grid=(batch, nt),
        in_specs=[
            pl.BlockSpec((tb, H_B * DK_B), tok),
            pl.BlockSpec((tb, H_B * DK_B), tok),
            pl.BlockSpec((tb, W_B), tok),
            pl.BlockSpec((tb, H_B * DK_B), tok),
            pl.BlockSpec((None, H_B, DK_B, DV_B), st),
            pl.BlockSpec(g.shape, lambda b, t: (0, 0)),
        ],
        out_specs=[
            pl.BlockSpec((tb, W_B), tok),
            pl.BlockSpec((None, H_B, DK_B, DV_B), st),
        ],
        out_shape=[
            jax.ShapeDtypeStruct((batch * seq, W_B), F32),
            jax.ShapeDtypeStruct((batch, H_B, DK_B, DV_B), F32),
        ],
        scratch_shapes=[pltpu.VMEM((H_B, DK_B, DV_B), F32)],
        compiler_params=_cparams("parallel", "arbitrary"),
        name="gla",
    )(qb, kb, vb, lg, s0, g)


def _mix_kernel(x_ref, oa_ref, ob_ref, gb_ref, wo_ref, g1_ref, b1_ref, wr_ref, br_ref, cnt0_ref,
                x1_ref, gw_ref, idx_ref, cnt_ref, cnt_scr):
    i = pl.program_id(0)

    @pl.when(i == 0)
    def _():
        cnt_scr[...] = cnt0_ref[...].astype(F32)

    rc = MIX_ROWS
    lane = lax.broadcasted_iota(I32, (rc, LANES), 1)
    lane_f = lane.astype(F32)
    big = float(LANES)
    neg = -jnp.inf
    r = lax.broadcasted_iota(I32, (rc, rc), 0)
    c = lax.broadcasted_iota(I32, (rc, rc), 1)
    earlier = jnp.where(c < r, 1.0, 0.0).astype(BF16)

    def first_argmax(v, vmax):
        return jnp.min(jnp.where(v == vmax, lane_f, big), axis=1, keepdims=True)

    cnt = cnt_scr[...]
    for ch in range(x_ref.shape[0] // rc):
        rows = slice(ch * rc, (ch + 1) * rc)
        gb = gb_ref[rows, :]
        obg = ob_ref[rows, :] * (gb * jax.nn.sigmoid(gb))
        mix = (_dot(oa_ref[rows, :].astype(BF16), wo_ref[0:W_A, :])
               + _dot(obg.astype(BF16), wo_ref[W_A:D_MODEL, :]))
        x1 = _layernorm(ALPHA * x_ref[rows, :] + mix, g1_ref[...], b1_ref[...])
        x1_ref[rows, :] = x1

        logits = _dot(x1, wr_ref[...], HI) + br_ref[...]
        lg1 = jnp.where((lane >= N_EXPERTS) & (lane < N_EXPERTS + N_GROUPS), logits, neg)
        m1 = jnp.max(lg1, axis=1, keepdims=True)
        pg = 1.0 / jnp.sum(jnp.exp(lg1 - m1), axis=1, keepdims=True)
        grp = first_argmax(lg1, m1) - float(N_EXPERTS)
        in_grp = (lane_f >= grp * EXP_PER_GROUP) & (lane_f < (grp + 1.0) * EXP_PER_GROUP)
        lg2 = jnp.where(in_grp, logits, neg)
        v1 = jnp.max(lg2, axis=1, keepdims=True)
        i1 = first_argmax(lg2, v1)
        lg2b = jnp.where(lane_f == i1, neg, lg2)
        v2 = jnp.max(lg2b, axis=1, keepdims=True)
        i2 = first_argmax(lg2b, v2)
        t = jnp.exp(v2 - v1)
        w1 = pg / (1.0 + t)
        w2 = pg * t / (1.0 + t)

        hit1 = lane_f == i1
        hit2 = lane_f == i2
        sel = jnp.where(hit1 | hit2, 1.0, 0.0)
        rank = _dot(earlier, sel.astype(BF16)) + cnt
        pos1 = jnp.sum(jnp.where(hit1, rank, 0.0), axis=1, keepdims=True)
        pos2 = jnp.sum(jnp.where(hit2, rank, 0.0), axis=1, keepdims=True)
        cnt = cnt + jnp.sum(sel, axis=0, keepdims=True)

        gw_ref[rows, :] = jnp.where(lane == 0, w1, jnp.where(lane == 1, w2, 0.0))
        idx_f = jnp.where(lane == 0, i1, jnp.where(lane == 1, i2, jnp.where(lane == 2, pos1,
                          jnp.where(lane == 3, pos2, 0.0))))
        idx_ref[rows, :] = idx_f.astype(I32)
    cnt_scr[...] = cnt
    cnt_ref[...] = cnt.astype(I32)


def _mix(x, oa, ob, gb, wo, g1, b1, wr, br, cnt0):
    n = x.shape[0]
    tm = TOKEN_TILE
    row = lambda i: (i, 0)
    fixed = lambda i: (0, 0)
    return pl.pallas_call(
        _mix_kernel,
        grid=(n // tm,),
        in_specs=[
            pl.BlockSpec((tm, D_MODEL), row),
            pl.BlockSpec((tm, W_A), row),
            pl.BlockSpec((tm, W_B), row),
            pl.BlockSpec((tm, W_B), row),
            pl.BlockSpec(wo.shape, fixed),
            pl.BlockSpec(g1.shape, fixed),
            pl.BlockSpec(b1.shape, fixed),
            pl.BlockSpec(wr.shape, fixed),
            pl.BlockSpec(br.shape, fixed),
            pl.BlockSpec((1, LANES), fixed),
        ],
        out_specs=[
            pl.BlockSpec((tm, D_MODEL), row),
            pl.BlockSpec((tm, LANES), row),
            pl.BlockSpec((tm, LANES), row),
            pl.BlockSpec((1, LANES), fixed),
        ],
        out_shape=[
            jax.ShapeDtypeStruct((n, D_MODEL), F32),
            jax.ShapeDtypeStruct((n, LANES), F32),
            jax.ShapeDtypeStruct((n, LANES), I32),
            jax.ShapeDtypeStruct((1, LANES), I32),
        ],
        scratch_shapes=[pltpu.VMEM((1, LANES), F32)],
        compiler_params=_cparams("arbitrary"),
        name="mix_route",
    )(x, oa, ob, gb, wo, g1, b1, wr, br, cnt0)


def _row_copy(src, dst, sem):
    return pltpu.make_async_copy(src, dst, sem)


def _dispatch_kernel(dst_ref, x_ref, buf_in_ref, buf_ref, sem):
    del buf_in_ref
    tm = x_ref.shape[0]
    for t in range(tm):
        for k in range(2):
            dst = dst_ref[0, 0, 2 * t + k]
            _row_copy(x_ref.at[pl.ds(t, 1), :], buf_ref.at[pl.ds(dst, 1), :], sem).start()
    for _ in range(2 * tm):
        _row_copy(x_ref.at[pl.ds(0, 1), :], buf_ref.at[pl.ds(0, 1), :], sem).wait()


def _dispatch(dst, x1, buf):
    n = x1.shape[0]
    n_rows = buf.shape[0]
    tm = TOKEN_TILE
    return pl.pallas_call(
        _dispatch_kernel,
        grid=(n // tm,),
        in_specs=[
            pl.BlockSpec((1, 1, 2 * tm), lambda i: (i, 0, 0), memory_space=pltpu.SMEM),
            pl.BlockSpec((tm, D_MODEL), lambda i: (i, 0)),
            pl.BlockSpec(memory_space=pl.ANY),
        ],
        out_specs=pl.BlockSpec(memory_space=pl.ANY),
        scratch_shapes=[pltpu.SemaphoreType.DMA(())],
        out_shape=jax.ShapeDtypeStruct((n_rows, D_MODEL), F32),
        input_output_aliases={2: 0},
        compiler_params=_cparams("arbitrary"),
        name="dispatch",
    )(dst, x1, buf)


def _expert_kernel(be_ref, nu_ref, x_ref, wg_ref, wu_ref, wd_ref, y_ref, wgb, wub, wdb):
    j = pl.program_id(0)

    @pl.when(j < nu_ref[0])
    def _():
        prev = be_ref[jnp.maximum(j - 1, 0)]

        @pl.when((j == 0) | (be_ref[j] != prev))
        def _():
            wgb[...] = wg_ref[...].astype(BF16)
            wub[...] = wu_ref[...].astype(BF16)
            wdb[...] = wd_ref[...].astype(BF16)

        x = x_ref[...].astype(BF16)
        hg = _dot(x, wgb[...])
        hu = _dot(x, wub[...])
        h = hg * jax.nn.sigmoid(hg) * hu
        y_ref[...] = _dot(h.astype(BF16), wdb[...])

    @pl.when(j >= nu_ref[0])
    def _():
        y_ref[...] = jnp.zeros(y_ref.shape, F32)


def _experts(block_e, n_used, xbuf, w_gate, w_up, w_down, *, li):
    n_rows = xbuf.shape[0]
    blk = EXPERT_BLOCK
    nb = n_rows // blk

    def rows(j, be, nu):
        return (jnp.minimum(j, nu[0] - 1), 0)

    def wsel(j, be, nu):
        return (li, be[jnp.minimum(j, nu[0] - 1)], 0, 0)

    grid_spec = pltpu.PrefetchScalarGridSpec(
        num_scalar_prefetch=2,
        grid=(nb,),
        in_specs=[
            pl.BlockSpec((blk, D_MODEL), rows),
            pl.BlockSpec((None, None, D_MODEL, D_EXPERT), wsel),
            pl.BlockSpec((None, None, D_MODEL, D_EXPERT), wsel),
            pl.BlockSpec((None, None, D_EXPERT, D_MODEL), wsel),
        ],
        out_specs=pl.BlockSpec((blk, D_MODEL), lambda j, be, nu: (j, 0)),
        scratch_shapes=[
            pltpu.VMEM((D_MODEL, D_EXPERT), BF16),
            pltpu.VMEM((D_MODEL, D_EXPERT), BF16),
            pltpu.VMEM((D_EXPERT, D_MODEL), BF16),
        ],
    )
    return pl.pallas_call(
        _expert_kernel,
        grid_spec=grid_spec,
        out_shape=jax.ShapeDtypeStruct((n_rows, D_MODEL), F32),
        compiler_params=_cparams("arbitrary"),
        name="experts",
    )(block_e, n_used, xbuf, w_gate, w_up, w_down)


def _combine_kernel(cur_ref, nxt_ref, gw_ref, x1_ref, p_ref, ybuf_ref, wpp_ref, wpg_ref,
                    g2_ref, b2_ref, gp_ref, o_ref, y_scr, sems, *, n_tiles):
    tm = x1_ref.shape[0]
    i = pl.program_id(0)
    slot = i & 1

    def gather(src_ref, s):
        for t in range(tm):
            for k in range(2):
                src = src_ref[0, 0, 2 * t + k]
                _row_copy(ybuf_ref.at[pl.ds(src, 1), :], y_scr.at[s, k, pl.ds(t, 1), :], sems.at[s]).start()

    @pl.when(i == 0)
    def _():
        gather(cur_ref, 0)

    if n_tiles > 1:
        @pl.when(i < n_tiles - 1)
        def _():
            gather(nxt_ref, 1 - slot)

    for _ in range(2 * tm):
        _row_copy(ybuf_ref.at[pl.ds(0, 1), :], y_scr.at[slot, 0, pl.ds(0, 1), :], sems.at[slot]).wait()

    gw = gw_ref[...]
    y = gw[:, 0:1] * y_scr[slot, 0] + gw[:, 1:2] * y_scr[slot, 1]
    x2 = _layernorm(ALPHA * x1_ref[...] + y, g2_ref[...], b2_ref[...])
    e = _rms(_dot(p_ref[...].astype(BF16), wpp_ref[...]), gp_ref[...])
    o_ref[...] = x2 + jax.nn.sigmoid(_dot(x2.astype(BF16), wpg_ref[...])) * e


def _combine(dst, gw, x1, p_l, ybuf, wpp, wpg, g2, b2, gp):
    n = x1.shape[0]
    tm = TOKEN_TILE
    nt = n // tm
    row = lambda i: (i, 0)
    fixed = lambda i: (0, 0)
    return pl.pallas_call(
        functools.partial(_combine_kernel, n_tiles=nt),
        grid=(nt,),
        in_specs=[
            pl.BlockSpec((1, 1, 2 * tm), lambda i: (i, 0, 0), memory_space=pltpu.SMEM),
            pl.BlockSpec((1, 1, 2 * tm), lambda i: (jnp.minimum(i + 1, nt - 1), 0, 0), memory_space=pltpu.SMEM),
            pl.BlockSpec((tm, LANES), row),
            pl.BlockSpec((tm, D_MODEL), row),
            pl.BlockSpec((tm, D_PLE), row),
            pl.BlockSpec(memory_space=pl.ANY),
            pl.BlockSpec(wpp.shape, fixed),
            pl.BlockSpec(wpg.shape, fixed),
            pl.BlockSpec(g2.shape, fixed),
            pl.BlockSpec(b2.shape, fixed),
            pl.BlockSpec(gp.shape, fixed),
        ],
        out_specs=pl.BlockSpec((tm, D_MODEL), row),
        scratch_shapes=[pltpu.VMEM((2, 2, tm, D_MODEL), F32), pltpu.SemaphoreType.DMA((2,))],
        out_shape=jax.ShapeDtypeStruct((n, D_MODEL), F32),
        compiler_params=_cparams("arbitrary"),
        name="combine",
    )(dst, dst, gw, x1, p_l, ybuf, wpp, wpg, g2, b2, gp)


def _rope_tables(pos):
    half = HD_A // 2
    inv = ROPE_THETA ** (-jnp.arange(half, dtype=F32) / half)
    ang = pos.astype(F32)[:, None] * inv[None, :]
    cos, sin = jnp.cos(ang), jnp.sin(ang)
    reps = LANES // HD_A
    return (jnp.tile(jnp.concatenate([cos, cos], axis=1), (1, reps)),
            jnp.tile(jnp.concatenate([-sin, sin], axis=1), (1, reps)))


def _moe_plan(counts, n_tokens):
    blk = EXPERT_BLOCK
    nb = (2 * n_tokens) // blk + N_EXPERTS
    padded = (counts + blk - 1) // blk * blk
    e = jnp.arange(N_EXPERTS, dtype=I32)
    pends = jnp.sum(jnp.where(e[None, :] <= e[:, None], padded[None, :], 0), axis=1)
    pstart = (pends - padded).astype(I32)
    first_row = jnp.arange(nb, dtype=I32) * blk
    block_e = jnp.minimum(jnp.sum(pends[None, :] <= first_row[:, None], axis=1), N_EXPERTS - 1).astype(I32)
    n_used = (pends[-1:] // blk).astype(I32)
    return pstart, block_e, n_used, nb * blk


def _mixers(x, lw, rope, attend, s0, cnt0, *, batch, seq, chunk, groups, kv_prev):
    cos_t, sin_t = rope
    qa, ka, qb, kb, vb, gb, lg, vt, k4, v4 = _inproj(
        x, lw["w_main"], lw["w_vt"], lw["w_glr"], lw["w_gk2"], lw["b_gk2"], cos_t, sin_t,
        groups=groups, kv_prev=kv_prev)
    oa = attend(qa, ka, vt, k4, v4)
    ob, s_out = _gla(qb, kb, vb, lg, s0, lw["gla_g"], batch=batch, seq=seq, chunk=chunk)
    x1, gw, idx, cnt = _mix(x, oa, ob, gb, lw["w_o"], lw["ln1_g"], lw["ln1_b"], lw["w_r"], lw["b_r"], cnt0)
    return (x1, gw, idx), cnt, k4, v4, s_out


def _slots(idx, pstart):
    eid, pos = idx[:, 0:2], idx[:, 2:4]
    first = jnp.sum(jnp.where(eid[:, :, None] == jnp.arange(N_EXPERTS, dtype=I32), pstart, 0), axis=-1)
    return (first + pos).reshape(idx.shape[0] // TOKEN_TILE, 1, 2 * TOKEN_TILE)


def _moe_and_embed(routed, cnt, p_ls, li, lw):
    n_tokens = sum(r[0].shape[0] for r in routed)
    pstart, block_e, n_used, n_rows = _moe_plan(cnt[0, :N_EXPERTS], n_tokens)
    dsts = [_slots(idx, pstart) for _, _, idx in routed]
    xbuf = jnp.zeros((n_rows, D_MODEL), F32)
    for (x1, _, _), dst in zip(routed, dsts):
        xbuf = _dispatch(dst, x1, xbuf)
    ybuf = _experts(block_e, n_used, xbuf, lw["w_gate"], lw["w_up"], lw["w_down"], li=li)
    return [_combine(dst, gw, x1, p_l, ybuf, lw["w_pp"], lw["w_pg"], lw["ln2_g"], lw["ln2_b"], lw["ple_g"])
            for (x1, gw, _), dst, p_l in zip(routed, dsts, p_ls)]


def kernel(x_prompt, x_sample, cache_k, cache_v, state_gla, page_table, p_prompt, p_sample, w_in, w_gk2, b_gk2, lam_q1, lam_k1, lam_q2, lam_k2, diff_norm_g, gla_norm_g, w_o, ln1_g, ln1_b, w_r1, b_r1, w_r2, b_r2, w_gate, w_up, w_down, ln2_g, ln2_b, w_ple_gate, w_ple_proj, ple_norm_g):
    bp, tp, _ = x_prompt.shape
    bs, ts, _ = x_sample.shape
    past = page_table.shape[1] * PAGE_SIZE
    rope_p = _rope_tables(jnp.arange(tp))
    rope_s = tuple(jnp.tile(t, (TOKEN_TILE // ts, 1)) for t in _rope_tables(past + jnp.arange(ts)))

    row2 = lambda a: a.reshape(1, -1)
    yp = x_prompt.reshape(bp * tp, D_MODEL)
    ys = x_sample.reshape(bs * ts, D_MODEL)
    s0_p = jnp.zeros((bp, H_B, DK_B, DV_B), F32)
    outs = {k: [] for k in ("sp", "ks", "vs", "ss")}
    kv_p = [None, None]
    for li in range(DEPTH):
        lw = {
            "w_main": w_in[li, :, :_OFF_GLR].astype(BF16),
            "w_vt": jnp.transpose(w_in[li, :, _OFF_VA:_OFF_VA + W_A]).astype(BF16),
            "w_glr": jnp.pad(w_in[li, :, _OFF_GLR:], ((0, 0), (0, LANES - GATE_RANK))).astype(BF16),
            "w_gk2": jnp.pad(w_gk2[li], ((0, LANES - GATE_RANK), (0, 0))),
            "b_gk2": row2(b_gk2[li]),
            "gla_g": row2(gla_norm_g[li]),
            "w_o": w_o[li].astype(BF16),
            "ln1_g": row2(ln1_g[li]), "ln1_b": row2(ln1_b[li]),
            "w_r": jnp.pad(jnp.concatenate([w_r2[li], w_r1[li]], axis=1),
                           ((0, 0), (0, LANES - N_EXPERTS - N_GROUPS))),
            "b_r": row2(jnp.pad(jnp.concatenate([b_r2[li], b_r1[li]]), (0, LANES - N_EXPERTS - N_GROUPS))),
            "w_gate": w_gate, "w_up": w_up, "w_down": w_down,
            "w_pp": w_ple_proj[li].astype(BF16),
            "w_pg": w_ple_gate[li].astype(BF16),
            "ln2_g": row2(ln2_g[li]), "ln2_b": row2(ln2_b[li]),
            "ple_g": row2(ple_norm_g[li]),
        }
        lamq = jnp.stack([lam_q1[li], lam_q2[li]])
        lamk = jnp.stack([lam_k1[li], lam_k2[li]])
        dg = row2(diff_norm_g[li])

        def attend_p(qa, ka, vt, k4, v4, lamq=lamq, lamk=lamk, dg=dg, li=li):
            del k4, v4
            return _prompt_attention(qa, ka, vt, lamq, lamk, dg, li=li, batch=bp, seq=tp)

        routed_p, cnt, kv_p[0], kv_p[1], s = _mixers(
            yp, lw, rope_p, attend_p, s0_p, jnp.zeros((1, LANES), I32),
            batch=bp, seq=tp, chunk=GLA_CHUNK, groups=bp, kv_prev=None if li == 0 else tuple(kv_p))
        outs["sp"].append(s)

        def attend_s(qa, ka, vt, k4, v4, lamq=lamq, lamk=lamk, dg=dg, li=li):
            del ka, vt
            return _sample_attention(qa, k4.reshape(-1, DKV_A), v4.reshape(-1, DKV_A), cache_k, cache_v,
                                     page_table, lamq, lamk, dg, li=li, batch=bs, ts=ts)

        routed_s, cnt, k4, v4, s = _mixers(ys, lw, rope_s, attend_s, state_gla[:, li], cnt,
                                           batch=bs, seq=ts, chunk=ts, groups=1, kv_prev=None)
        outs["ks"].append(k4.reshape(bs, ts, H_A, DKV_A))
        outs["vs"].append(v4.reshape(bs, ts, H_A, DKV_A))
        outs["ss"].append(s)

        yp, ys = _moe_and_embed(
            [routed_p, routed_s], cnt,
            [p_prompt[li].reshape(bp * tp, D_PLE), p_sample[li].reshape(bs * ts, D_PLE)], li, lw)

    stack = lambda k: jnp.stack(outs[k], axis=1)
    return (yp.reshape(bp, tp, D_MODEL), ys.reshape(bs, ts, D_MODEL),
            kv_p[0].reshape(bp, DEPTH, tp, H_A, DKV_A), kv_p[1].reshape(bp, DEPTH, tp, H_A, DKV_A),
            stack("sp"), stack("ks"), stack("vs"), stack("ss"))
```

```python
import functools
import math

import jax
import jax.numpy as jnp
from jax import lax
from jax.experimental import pallas as pl
from jax.experimental.pallas import tpu as pltpu

F32 = jnp.float32
BF16 = jnp.bfloat16
I32 = jnp.int32
HI = lax.Precision.HIGHEST

D_MODEL = 1024
DEPTH = 2
PAGE_SIZE = 128
D_PLE = 256
HD_A = 64
DKV_A = 2 * HD_A
W_A = D_MODEL // 2
H_A = W_A // DKV_A
W_B = D_MODEL - W_A
H_B = 4
DV_B = W_B // H_B
DK_B = DV_B // 2
GATE_RANK = 16
GATE_NORM = 16.0
GLA_CHUNK = 64
N_GROUPS = 4
EXP_PER_GROUP = 8
N_EXPERTS = N_GROUPS * EXP_PER_GROUP
D_EXPERT = D_MODEL // 2
ROPE_THETA = 10000.0
ALPHA = (2 * DEPTH) ** 0.25
EPS = 1e-5
LOG2E = math.log2(math.e)

LANES = 128
VMEM_LIMIT = 48 * 1024 * 1024

TOKEN_TILE = 256
ATTN_TILE = 256
PAGES_PER_STEP = 8
PAGE_SLOTS = 3
PAGE_GROUP = 4
GLA_TILE = 256
EXPERT_BLOCK = 256
MIX_ROWS = 128

_OFF_QA, _OFF_KA, _OFF_VA = 0, W_A, 2 * W_A
_OFF_QB = 3 * W_A
_OFF_KB = _OFF_QB + H_B * DK_B
_OFF_VB = _OFF_KB + H_B * DK_B
_OFF_GB = _OFF_VB + W_B
_OFF_GLR = _OFF_GB + W_B

_NT = (((1,), (1,)), ((), ()))
_TN = (((0,), (0,)), ((), ()))


def _cparams(*sem):
    return pltpu.CompilerParams(dimension_semantics=sem, vmem_limit_bytes=VMEM_LIMIT)


def _dot(a, b, precision=None):
    return jnp.dot(a, b, preferred_element_type=F32, precision=precision)


def _lam_init(li):
    return 0.8 - 0.6 * math.exp(-0.3 * li)


def _lam(lamq_ref, lamk_ref, li):
    s = jnp.sum(lamq_ref[...] * lamk_ref[...], axis=1, keepdims=True)
    e = jnp.exp(s)
    return e[0:1, :] - e[1:2, :] + _lam_init(li)


def _rms(x, g):
    return x * lax.rsqrt(jnp.mean(x * x, axis=-1, keepdims=True) + EPS) * g


def _layernorm(x, g, b):
    mu = jnp.mean(x, axis=-1, keepdims=True)
    xc = x - mu
    var = jnp.mean(xc * xc, axis=-1, keepdims=True)
    return xc * lax.rsqrt(var + EPS) * g + b


def _inproj_kernel(x_ref, w_ref, wvt_ref, wglr_ref, wgk2_ref, bgk2_ref, cos_ref, sin_ref, *refs):
    qa_ref, ka_ref, qb_ref, kb_ref, vb_ref, gb_ref, lg_ref, vt_ref, k4_ref, v4_ref = refs[-10:]
    tm = x_ref.shape[0]
    layer = k4_ref.shape[0] - 1
    if len(refs) > 10:
        k4_ref[0:layer] = refs[0][...]
        v4_ref[0:layer] = refs[1][...]
    xb = x_ref[...].astype(BF16)
    vt_ref[...] = lax.dot_general(wvt_ref[...], xb, _NT, preferred_element_type=F32).astype(BF16)
    cos = cos_ref[...]
    sin = sin_ref[...]
    lane = lax.broadcasted_iota(I32, cos.shape, 1)
    first_half = (lane & (HD_A // 2)) == 0

    def rope(h):
        partner = jnp.where(first_half, pltpu.roll(h, LANES - HD_A // 2, 1), pltpu.roll(h, HD_A // 2, 1))
        return h * cos + partner * sin

    for c in range(W_A // LANES):
        lo = c * LANES
        hq = _dot(xb, w_ref[:, _OFF_QA + lo:_OFF_QA + lo + LANES])
        qa_ref[:, lo:lo + LANES] = rope(hq) * (HD_A ** -0.5 * LOG2E)
        hk = rope(_dot(xb, w_ref[:, _OFF_KA + lo:_OFF_KA + lo + LANES]))
        ka_ref[:, lo:lo + LANES] = hk
        k4_ref[layer, pl.ds(c, tm, stride=H_A), :] = hk
        v4_ref[layer, pl.ds(c, tm, stride=H_A), :] = _dot(xb, w_ref[:, _OFF_VA + lo:_OFF_VA + lo + LANES])
    qb_ref[...] = _dot(xb, w_ref[:, _OFF_QB:_OFF_KB]) * (DK_B ** -0.5)
    kb_ref[...] = _dot(xb, w_ref[:, _OFF_KB:_OFF_VB])
    vb_ref[...] = _dot(xb, w_ref[:, _OFF_VB:_OFF_GB])
    gb_ref[...] = _dot(xb, w_ref[:, _OFF_GB:_OFF_GLR])
    glr = _dot(xb, wglr_ref[...])
    z = _dot(glr, wgk2_ref[...], HI) + bgk2_ref[...]
    lg_ref[...] = (jnp.minimum(z, 0.0) - jnp.log1p(jnp.exp(-jnp.abs(z)))) * (1.0 / GATE_NORM)


def _inproj(x, w_main, w_vt, w_glr, w_gk2p, b_gk2, cos_t, sin_t, *, groups, kv_prev=None):
    n = x.shape[0]
    tm = TOKEN_TILE
    ntab = cos_t.shape[0] // tm
    nq = n // tm // groups
    kv_idx = lambda i: (i // nq, 0, i % nq, 0)
    layers = 1 if kv_prev is None else kv_prev[0].shape[1] + 1
    row = lambda i: (i, 0)
    fixed = lambda i: (0, 0)
    tab = lambda i: (i % ntab, 0)
    widths = (W_A, W_A, H_B * DK_B, H_B * DK_B, W_B, W_B, H_B * DK_B)
    operands = [x, w_main, w_vt, w_glr, w_gk2p, b_gk2, cos_t, sin_t]
    in_specs = [
        pl.BlockSpec((tm, D_MODEL), row),
        pl.BlockSpec(w_main.shape, fixed),
        pl.BlockSpec(w_vt.shape, fixed),
        pl.BlockSpec(w_glr.shape, fixed),
        pl.BlockSpec(w_gk2p.shape, fixed),
        pl.BlockSpec(b_gk2.shape, fixed),
        pl.BlockSpec((tm, LANES), tab),
        pl.BlockSpec((tm, LANES), tab),
    ]
    if kv_prev is not None:
        operands += list(kv_prev)
        in_specs += [pl.BlockSpec((None, layers - 1, tm * H_A, DKV_A), kv_idx)] * 2
    kv_shape = jax.ShapeDtypeStruct((groups, layers, nq * tm * H_A, DKV_A), F32)
    return pl.pallas_call(
        _inproj_kernel,
        grid=(n // tm,),
        in_specs=in_specs,
        out_specs=[pl.BlockSpec((tm, w), row) for w in widths]
        + [pl.BlockSpec((None, W_A, tm), lambda i: (i, 0, 0))]
        + [pl.BlockSpec((None, layers, tm * H_A, DKV_A), kv_idx)] * 2,
        out_shape=[jax.ShapeDtypeStruct((n, w), F32) for w in widths]
        + [jax.ShapeDtypeStruct((n // tm, W_A, tm), BF16), kv_shape, kv_shape],
        compiler_params=_cparams("parallel"),
        name="inproj",
    )(*operands)


def _pattn_kernel(lamq_ref, lamk_ref, g_ref, q_ref, k_ref, vt_ref, o_ref, acc_scr, sa_scr, sb_scr, *, li):
    tq = q_ref.shape[0]
    win = 2 * tq
    last_win = k_ref.shape[0] // win - 1
    qi = pl.program_id(2)
    n_pairs = lax.shift_right_logical(qi, 2)
    q = q_ref[...]
    lane = lax.broadcasted_iota(I32, q.shape, 1)
    qc = (jnp.where(lane < HD_A, q, 0.0).astype(BF16), jnp.where(lane >= HD_A, q, 0.0).astype(BF16))
    acc_scr[...] = jnp.zeros(acc_scr.shape, F32)

    def scores(w, s_scr):
        w = jnp.minimum(w, last_win)
        start = pl.multiple_of(w * win, win)
        kb = k_ref[pl.ds(start, win), :].astype(BF16)
        for c in range(2):
            s_scr[c] = lax.dot_general(kb, qc[c], _NT, preferred_element_type=F32)

    def consume(w, s_scr, stats, masked):
        vt0 = vt_ref[2 * w]
        vt1 = vt_ref[2 * w + 1]
        out = []
        for c in range(2):
            m_prev, l_prev = stats[c]
            st = s_scr[c]
            if masked:
                key = lax.broadcasted_iota(I32, st.shape, 0)
                qry = lax.broadcasted_iota(I32, (1, tq), 1) + (qi * tq - w * win)
                st = jnp.where(key <= qry, st, -jnp.inf)
            m_new = jnp.maximum(m_prev, jnp.max(st, axis=0, keepdims=True))
            alpha = jnp.exp2(m_prev - m_new)
            p = jnp.exp2(st - m_new)
            l_new = alpha * l_prev + jnp.sum(p, axis=0, keepdims=True)
            p = p.astype(BF16)
            cols = slice(c * tq, (c + 1) * tq)
            acc_scr[:, cols] = alpha * acc_scr[:, cols] + (_dot(vt0, p[0:tq]) + _dot(vt1, p[tq:win]))
            out.append((m_new, l_new))
        return tuple(out)

    def pair(i, stats):
        scores(2 * i + 1, sb_scr)
        stats = consume(2 * i, sa_scr, stats, False)
        scores(2 * i + 2, sa_scr)
        return consume(2 * i + 1, sb_scr, stats, False)

    def last_pair(i, stats):
        def both(s):
            scores(2 * i + 1, sb_scr)
            s = consume(2 * i, sa_scr, s, True)
            return consume(2 * i + 1, sb_scr, s, True)

        def first_only(s):
            return consume(2 * i, sa_scr, s, True)

        reaches_second = (qi & 3) >= 2
        return lax.cond(reaches_second, both, first_only, stats)

    scores(0, sa_scr)
    stat = (jnp.full((1, tq), -jnp.inf, F32), jnp.zeros((1, tq), F32))
    stats = lax.fori_loop(0, n_pairs, pair, (stat, stat))
    (_, l1), (_, l2) = last_pair(n_pairs, stats)
    out_t = acc_scr[:, 0:tq] / l1 - _lam(lamq_ref, lamk_ref, li) * (acc_scr[:, tq:2 * tq] / l2)
    o_ref[...] = _rms(jnp.transpose(out_t), g_ref[...]) * (1.0 - _lam_init(li))


def _prompt_attention(qa, ka, vt, lamq, lamk, g, *, li, batch, seq):
    tq = ATTN_TILE
    assert tq == TOKEN_TILE and seq % (4 * tq) == 0
    nq = seq // tq
    small = lambda b, h, i: (0, 0)
    return pl.pallas_call(
        functools.partial(_pattn_kernel, li=li),
        grid=(batch, H_A, nq),
        in_specs=[
            pl.BlockSpec(lamq.shape, small),
            pl.BlockSpec(lamk.shape, small),
            pl.BlockSpec(g.shape, small),
            pl.BlockSpec((tq, DKV_A), lambda b, h, i: (b * nq + i, h)),
            pl.BlockSpec((seq, DKV_A), lambda b, h, i: (b, h)),
            pl.BlockSpec((nq, DKV_A, tq), lambda b, h, i: (b, h, 0)),
        ],
        out_specs=pl.BlockSpec((tq, DKV_A), lambda b, h, i: (b * nq + i, h)),
        out_shape=jax.ShapeDtypeStruct(qa.shape, F32),
        scratch_shapes=[pltpu.VMEM((DKV_A, 2 * tq), F32),
                        pltpu.VMEM((2, 2 * tq, tq), F32),
                        pltpu.VMEM((2, 2 * tq, tq), F32)],
        compiler_params=_cparams("parallel", "parallel", "arbitrary"),
        name="prompt_attn",
    )(lamq, lamk, g, qa, ka, vt)


def _sattn_kernel(pt_ref, lamq_ref, lamk_ref, g_ref, q_ref, kn_ref, vn_ref, ck_ref, cv_ref, o_ref,
                  qt_scr, m_scr, l_scr, acc_scr, kn_scr, vn_scr, kbuf, vbuf, sems, *, li, npages, ts):
    step = pl.program_id(1)
    steps_per_seq = pl.num_programs(1)
    n_steps = pl.num_programs(0) * steps_per_seq
    lin = pl.program_id(0) * steps_per_seq + step
    slot = lin % PAGE_SLOTS
    nrow = H_A * 2 * ts
    rows_per_page = PAGE_SIZE * H_A

    def page_copies(t, s):
        b = t // steps_per_seq
        g = t % steps_per_seq
        out = []
        for i in range(npages):
            page = pt_ref[b, g * npages + i]
            out.append(pltpu.make_async_copy(ck_ref.at[page, li], kbuf.at[s, i], sems.at[s]))
            out.append(pltpu.make_async_copy(cv_ref.at[page, li], vbuf.at[s, i], sems.at[s]))
        return out

    @pl.when(lin == 0)
    def _():
        for t in range(PAGE_SLOTS - 1):
            for cp in page_copies(t, t):
                cp.start()

    ahead = lin + (PAGE_SLOTS - 1)

    @pl.when(ahead < n_steps)
    def _():
        for cp in page_copies(ahead, ahead % PAGE_SLOTS):
            cp.start()

    def head_match(shape):
        row = lax.broadcasted_iota(I32, shape, 0)
        col = lax.broadcasted_iota(I32, shape, 1)
        return (col & (H_A - 1)) == (row // (2 * ts)), row, col

    def update(state, scores, values):
        m_prev, l_prev, acc = state
        m_new = m_prev
        for s in scores:
            m_new = jnp.maximum(m_new, jnp.max(s, axis=1, keepdims=True))
        alpha = jnp.exp2(m_prev - m_new)
        l_new = alpha * l_prev
        acc = alpha * acc
        for s, v_bf in zip(scores, values):
            p = jnp.exp2(s - m_new)
            l_new = l_new + jnp.sum(p, axis=1, keepdims=True)
            acc = acc + _dot(p.astype(BF16), v_bf)
        return m_new, l_new, acc

    @pl.when(step == 0)
    def _():
        q = q_ref[...]
        lane = lax.broadcasted_iota(I32, (ts, DKV_A), 1)
        blocks = []
        for h in range(H_A):
            qh = q[:, h * DKV_A:(h + 1) * DKV_A]
            blocks += [jnp.where(lane < HD_A, qh, 0.0), jnp.where(lane >= HD_A, qh, 0.0)]
        qt = jnp.concatenate(blocks, axis=0).astype(BF16)
        qt_scr[...] = qt
        kn_scr[...] = jnp.zeros(kn_scr.shape, F32)
        vn_scr[...] = jnp.zeros(vn_scr.shape, F32)
        kn_scr[0:ts * H_A, :] = kn_ref[...]
        vn_scr[0:ts * H_A, :] = vn_ref[...]
        s = lax.dot_general(qt, kn_scr[...].astype(BF16), _NT, preferred_element_type=F32)
        match, row, col = head_match(s.shape)
        ok = match & ((col // H_A) <= (row & (ts - 1))) & (col < ts * H_A)
        s = jnp.where(ok, s, -jnp.inf)
        init = (jnp.full((nrow, 1), -jnp.inf, F32), jnp.zeros((nrow, 1), F32), jnp.zeros((nrow, DKV_A), F32))
        m, l, acc = update(init, [s], [vn_scr[...].astype(BF16)])
        m_scr[...] = jnp.broadcast_to(m, m_scr.shape)
        l_scr[...] = jnp.broadcast_to(l, l_scr.shape)
        acc_scr[...] = acc

    for cp in page_copies(lin, slot):
        cp.wait()

    qt = qt_scr[...]
    match, _, _ = head_match((nrow, rows_per_page))
    scores = [jnp.where(match, lax.dot_general(qt, kbuf[slot, i].astype(BF16), _NT, preferred_element_type=F32),
                        -jnp.inf) for i in range(npages)]
    state = (m_scr[:, 0:1], l_scr[:, 0:1], acc_scr[...])
    for g0 in range(0, npages, PAGE_GROUP):
        state = update(state, scores[g0:g0 + PAGE_GROUP],
                       [vbuf[slot, i].astype(BF16) for i in range(g0, g0 + PAGE_GROUP)])
    m, l, acc = state
    m_scr[...] = jnp.broadcast_to(m, m_scr.shape)
    l_scr[...] = jnp.broadcast_to(l, l_scr.shape)
    acc_scr[...] = acc

    @pl.when(step == steps_per_seq - 1)
    def _():
        o = acc_scr[...] / l_scr[:, 0:1]
        lam = _lam(lamq_ref, lamk_ref, li)
        for h in range(H_A):
            r0 = h * 2 * ts
            out = o[r0:r0 + ts, :] - lam * o[r0 + ts:r0 + 2 * ts, :]
            o_ref[:, h * DKV_A:(h + 1) * DKV_A] = _rms(out, g_ref[...]) * (1.0 - _lam_init(li))


def _sample_attention(qa, kn, vn, cache_k, cache_v, page_table, lamq, lamk, g, *, li, batch, ts):
    npg = PAGES_PER_STEP
    n_pages = page_table.shape[1]
    assert n_pages % npg == 0 and npg % PAGE_GROUP == 0 and ts == 8 and ts * H_A <= PAGE_SIZE
    assert batch * (n_pages // npg) >= PAGE_SLOTS
    nrow = H_A * 2 * ts
    rows_per_page = PAGE_SIZE * H_A
    small = lambda b, s, pt: (0, 0)
    new = lambda b, s, pt: (b, 0)
    grid_spec = pltpu.PrefetchScalarGridSpec(
        num_scalar_prefetch=1,
        grid=(batch, n_pages // npg),
        in_specs=[
            pl.BlockSpec(lamq.shape, small),
            pl.BlockSpec(lamk.shape, small),
            pl.BlockSpec(g.shape, small),
            pl.BlockSpec((ts, W_A), new),
            pl.BlockSpec((ts * H_A, DKV_A), new),
            pl.BlockSpec((ts * H_A, DKV_A), new),
            pl.BlockSpec(memory_space=pl.ANY),
            pl.BlockSpec(memory_space=pl.ANY),
        ],
        out_specs=pl.BlockSpec((ts, W_A), new),
        scratch_shapes=[
            pltpu.VMEM((nrow, DKV_A), BF16),
            pltpu.VMEM((nrow, LANES), F32),
            pltpu.VMEM((nrow, LANES), F32),
            pltpu.VMEM((nrow, DKV_A), F32),
            pltpu.VMEM((PAGE_SIZE, DKV_A), F32),
            pltpu.VMEM((PAGE_SIZE, DKV_A), F32),
            pltpu.VMEM((PAGE_SLOTS, npg, rows_per_page, DKV_A), F32),
            pltpu.VMEM((PAGE_SLOTS, npg, rows_per_page, DKV_A), F32),
            pltpu.SemaphoreType.DMA((PAGE_SLOTS,)),
        ],
    )
    ck = cache_k.reshape(cache_k.shape[0], DEPTH, rows_per_page, DKV_A)
    cv = cache_v.reshape(cache_v.shape[0], DEPTH, rows_per_page, DKV_A)
    return pl.pallas_call(
        functools.partial(_sattn_kernel, li=li, npages=npg, ts=ts),
        grid_spec=grid_spec,
        out_shape=jax.ShapeDtypeStruct(qa.shape, F32),
        compiler_params=_cparams("arbitrary", "arbitrary"),
        name="sample_attn",
    )(page_table, lamq, lamk, g, qa, kn, vn, ck, cv)


def _gla_kernel(q_ref, k_ref, v_ref, lg_ref, s0_ref, g_ref, o_ref, sout_ref, s_scr, *, chunk):
    t = pl.program_id(1)

    @pl.when(t == 0)
    def _():
        s_scr[...] = s0_ref[...]

    tile = q_ref.shape[0]
    n_chunks = tile // chunk
    row = lax.broadcasted_iota(I32, (tile, tile), 0)
    col = lax.broadcasted_iota(I32, (tile, tile), 1)
    tri = ((col <= row) & ((row // chunk) == (col // chunk))).astype(F32)
    causal = (lax.broadcasted_iota(I32, (chunk, chunk), 1) <= lax.broadcasted_iota(I32, (chunk, chunk), 0))

    bc = _dot(tri, lg_ref[...], HI)
    bl = jnp.concatenate([jnp.broadcast_to(bc[(c + 1) * chunk - 1:(c + 1) * chunk, :], (chunk, H_B * DK_B))
                          for c in range(n_chunks)], axis=0)
    k_all = k_ref[...]
    qg_all = (q_ref[...] * jnp.exp(bc)).astype(BF16)
    kg_all = (k_all * jnp.exp(-bc)).astype(BF16)
    kd_all = (k_all * jnp.exp(bl - bc)).astype(BF16)
    v_all = v_ref[...].astype(BF16)
    intra, incr, decay = {}, {}, {}
    for c in range(n_chunks):
        rows = slice(c * chunk, (c + 1) * chunk)
        decay[c] = jnp.transpose(jnp.broadcast_to(jnp.exp(bl[c * chunk:c * chunk + 1, :]),
                                                  (DV_B, H_B * DK_B)))
        for h in range(H_B):
            dk = slice(h * DK_B, (h + 1) * DK_B)
            dv = slice(h * DV_B, (h + 1) * DV_B)
            a = lax.dot_general(qg_all[rows, dk], kg_all[rows, dk], _NT, preferred_element_type=F32)
            intra[c, h] = jnp.where(causal, a, 0.0).astype(BF16)
            incr[c, h] = lax.dot_general(kd_all[rows, dk], v_all[rows, dv], _TN, preferred_element_type=F32)

    for h in range(H_B):
        dk = slice(h * DK_B, (h + 1) * DK_B)
        dv = slice(h * DV_B, (h + 1) * DV_B)
        s_cur = s_scr[h]
        for c in range(n_chunks):
            rows = slice(c * chunk, (c + 1) * chunk)
            o = _dot(qg_all[rows, dk], s_cur.astype(BF16)) + _dot(intra[c, h], v_all[rows, dv])
            s_cur = decay[c][dk, :] * s_cur + incr[c, h]
            o_ref[rows, dv] = _rms(o, g_ref[...])
        s_scr[h] = s_cur

    @pl.when(t == pl.num_programs(1) - 1)
    def _():
        sout_ref[...] = s_scr[...]


def _gla(qb, kb, vb, lg, s0, g, *, batch, seq, chunk):
    tb = min(seq, GLA_TILE)
    assert seq % tb == 0 and tb % chunk == 0
    nt = seq // tb
    tok = lambda b, t: (b * nt + t, 0)
    st = lambda b, t: (b, 0, 0, 0)
    return pl.pallas_call(
        functools.partial(_gla_kernel, chunk=chunk),
        grid=(batch, nt),
        in_specs=[
            pl.BlockSpec((tb, H_B * DK_B), tok),
            pl.BlockSpec((tb, H_B * DK_B), tok),
            pl.BlockSpec((tb, W_B), tok),
            pl.BlockSpec((tb, H_B * DK_B), tok),
            pl.BlockSpec((None, H_B, DK_B, DV_B), st),
            pl.BlockSpec(g.shape, lambda b, t: (0, 0)),
        ],
        out_specs=[
            pl.BlockSpec((tb, W_B), tok),
            pl.BlockSpec((None, H_B, DK_B, DV_B), st),
        ],
        out_shape=[
            jax.ShapeDtypeStruct((batch * seq, W_B), F32),
            jax.ShapeDtypeStruct((batch, H_B, DK_B, DV_B), F32),
        ],
        scratch_shapes=[pltpu.VMEM((H_B, DK_B, DV_B), F32)],
        compiler_params=_cparams("parallel", "arbitrary"),
        name="gla",
    )(qb, kb, vb, lg, s0, g)


def _mix_kernel(x_ref, oa_ref, ob_ref, gb_ref, wo_ref, g1_ref, b1_ref, wr_ref, br_ref, cnt0_ref,
                x1_ref, gw_ref, idx_ref, cnt_ref, cnt_scr):
    i = pl.program_id(0)

    @pl.when(i == 0)
    def _():
        cnt_scr[...] = cnt0_ref[...].astype(F32)

    rc = MIX_ROWS
    lane = lax.broadcasted_iota(I32, (rc, LANES), 1)
    lane_f = lane.astype(F32)
    big = float(LANES)
    neg = -jnp.inf
    r = lax.broadcasted_iota(I32, (rc, rc), 0)
    c = lax.broadcasted_iota(I32, (rc, rc), 1)
    earlier = jnp.where(c < r, 1.0, 0.0).astype(BF16)

    def first_argmax(v, vmax):
        return jnp.min(jnp.where(v == vmax, lane_f, big), axis=1, keepdims=True)

    chunks = [slice(ch * rc, (ch + 1) * rc) for ch in range(x_ref.shape[0] // rc)]
    x1s = []
    for rows in chunks:
        gb = gb_ref[rows, :]
        obg = ob_ref[rows, :] * (gb * jax.nn.sigmoid(gb))
        mix = (_dot(oa_ref[rows, :].astype(BF16), wo_ref[0:W_A, :])
               + _dot(obg.astype(BF16), wo_ref[W_A:D_MODEL, :]))
        x1 = _layernorm(ALPHA * x_ref[rows, :] + mix, g1_ref[...], b1_ref[...])
        x1_ref[rows, :] = x1
        x1s.append(x1)
    all_logits = [_dot(x1, wr_ref[...], HI) + br_ref[...] for x1 in x1s]

    cnt = cnt_scr[...]
    for rows, logits in zip(chunks, all_logits):
        lg1 = jnp.where((lane >= N_EXPERTS) & (lane < N_EXPERTS + N_GROUPS), logits, neg)
        m1 = jnp.max(lg1, axis=1, keepdims=True)
        pg = 1.0 / jnp.sum(jnp.exp(lg1 - m1), axis=1, keepdims=True)
        grp = first_argmax(lg1, m1) - float(N_EXPERTS)
        in_grp = (lane_f >= grp * EXP_PER_GROUP) & (lane_f < (grp + 1.0) * EXP_PER_GROUP)
        lg2 = jnp.where(in_grp, logits, neg)
        v1 = jnp.max(lg2, axis=1, keepdims=True)
        i1 = first_argmax(lg2, v1)
        lg2b = jnp.where(lane_f == i1, neg, lg2)
        v2 = jnp.max(lg2b, axis=1, keepdims=True)
        i2 = first_argmax(lg2b, v2)
        t = jnp.exp(v2 - v1)
        w1 = pg / (1.0 + t)
        w2 = pg * t / (1.0 + t)

        hit1 = lane_f == i1
        hit2 = lane_f == i2
        sel = jnp.where(hit1 | hit2, 1.0, 0.0)
        rank = _dot(earlier, sel.astype(BF16)) + cnt
        pos1 = jnp.sum(jnp.where(hit1, rank, 0.0), axis=1, keepdims=True)
        pos2 = jnp.sum(jnp.where(hit2, rank, 0.0), axis=1, keepdims=True)
        cnt = cnt + jnp.sum(sel, axis=0, keepdims=True)

        gw_ref[rows, :] = jnp.where(lane == 0, w1, jnp.where(lane == 1, w2, 0.0))
        idx_f = jnp.where(lane == 0, i1, jnp.where(lane == 1, i2, jnp.where(lane == 2, pos1,
                          jnp.where(lane == 3, pos2, 0.0))))
        idx_ref[rows, :] = idx_f.astype(I32)
    cnt_scr[...] = cnt
    cnt_ref[...] = cnt.astype(I32)


def _mix(x, oa, ob, gb, wo, g1, b1, wr, br, cnt0):
    n = x.shape[0]
    tm = TOKEN_TILE
    row = lambda i: (i, 0)
    fixed = lambda i: (0, 0)
    return pl.pallas_call(
        _mix_kernel,
        grid=(n // tm,),
        in_specs=[
            pl.BlockSpec((tm, D_MODEL), row),
            pl.BlockSpec((tm, W_A), row),
            pl.BlockSpec((tm, W_B), row),
            pl.BlockSpec((tm, W_B), row),
            pl.BlockSpec(wo.shape, fixed),
            pl.BlockSpec(g1.shape, fixed),
            pl.BlockSpec(b1.shape, fixed),
            pl.BlockSpec(wr.shape, fixed),
            pl.BlockSpec(br.shape, fixed),
            pl.BlockSpec((1, LANES), fixed),
        ],
        out_specs=[
            pl.BlockSpec((tm, D_MODEL), row),
            pl.BlockSpec((tm, LANES), row),
            pl.BlockSpec((tm, LANES), row),
            pl.BlockSpec((1, LANES), fixed),
        ],
        out_shape=[
            jax.ShapeDtypeStruct((n, D_MODEL), F32),
            jax.ShapeDtypeStruct((n, LANES), F32),
            jax.ShapeDtypeStruct((n, LANES), I32),
            jax.ShapeDtypeStruct((1, LANES), I32),
        ],
        scratch_shapes=[pltpu.VMEM((1, LANES), F32)],
        compiler_params=_cparams("arbitrary"),
        name="mix_route",
    )(x, oa, ob, gb, wo, g1, b1, wr, br, cnt0)


def _row_copy(src, dst, sem):
    return pltpu.make_async_copy(src, dst, sem)


def _dispatch_kernel(dst_ref, x_ref, buf_in_ref, buf_ref, sem):
    del buf_in_ref
    tm = x_ref.shape[0]
    for t in range(tm):
        for k in range(2):
            dst = dst_ref[0, 0, 2 * t + k]
            _row_copy(x_ref.at[pl.ds(t, 1), :], buf_ref.at[pl.ds(dst, 1), :], sem).start()
    for _ in range(2 * tm):
        _row_copy(x_ref.at[pl.ds(0, 1), :], buf_ref.at[pl.ds(0, 1), :], sem).wait()


def _dispatch(dst, x1, buf):
    n = x1.shape[0]
    n_rows = buf.shape[0]
    tm = TOKEN_TILE
    return pl.pallas_call(
        _dispatch_kernel,
        grid=(n // tm,),
        in_specs=[
            pl.BlockSpec((1, 1, 2 * tm), lambda i: (i, 0, 0), memory_space=pltpu.SMEM),
            pl.BlockSpec((tm, D_MODEL), lambda i: (i, 0)),
            pl.BlockSpec(memory_space=pl.ANY),
        ],
        out_specs=pl.BlockSpec(memory_space=pl.ANY),
        scratch_shapes=[pltpu.SemaphoreType.DMA(())],
        out_shape=jax.ShapeDtypeStruct((n_rows, D_MODEL), F32),
        input_output_aliases={2: 0},
        compiler_params=_cparams("arbitrary"),
        name="dispatch",
    )(dst, x1, buf)


def _expert_kernel(be_ref, nu_ref, x_ref, wg_ref, wu_ref, wd_ref, y_ref, wgb, wub, wdb):
    j = pl.program_id(0)

    @pl.when(j < nu_ref[0])
    def _():
        prev = be_ref[jnp.maximum(j - 1, 0)]

        @pl.when((j == 0) | (be_ref[j] != prev))
        def _():
            wgb[...] = wg_ref[...].astype(BF16)
            wub[...] = wu_ref[...].astype(BF16)
            wdb[...] = wd_ref[...].astype(BF16)

        half = x_ref.shape[0] // 2
        halves = (slice(0, half), slice(half, 2 * half))
        xs = [x_ref[r, :].astype(BF16) for r in halves]
        hg = [_dot(x, wgb[...]) for x in xs]
        hu = [_dot(x, wub[...]) for x in xs]
        for r, g, u in zip(halves, hg, hu):
            h = g * jax.nn.sigmoid(g) * u
            y_ref[r, :] = _dot(h.astype(BF16), wdb[...])

    @pl.when(j >= nu_ref[0])
    def _():
        y_ref[...] = jnp.zeros(y_ref.shape, F32)


def _experts(block_e, n_used, xbuf, w_gate, w_up, w_down, *, li):
    n_rows = xbuf.shape[0]
    blk = EXPERT_BLOCK
    nb = n_rows // blk

    def rows(j, be, nu):
        return (jnp.minimum(j, nu[0] - 1), 0)

    def wsel(j, be, nu):
        return (li, be[jnp.minimum(j, nu[0] - 1)], 0, 0)

    grid_spec = pltpu.PrefetchScalarGridSpec(
        num_scalar_prefetch=2,
        grid=(nb,),
        in_specs=[
            pl.BlockSpec((blk, D_MODEL), rows),
            pl.BlockSpec((None, None, D_MODEL, D_EXPERT), wsel),
            pl.BlockSpec((None, None, D_MODEL, D_EXPERT), wsel),
            pl.BlockSpec((None, None, D_EXPERT, D_MODEL), wsel),
        ],
        out_specs=pl.BlockSpec((blk, D_MODEL), lambda j, be, nu: (j, 0)),
        scratch_shapes=[
            pltpu.VMEM((D_MODEL, D_EXPERT), BF16),
            pltpu.VMEM((D_MODEL, D_EXPERT), BF16),
            pltpu.VMEM((D_EXPERT, D_MODEL), BF16),
        ],
    )
    return pl.pallas_call(
        _expert_kernel,
        grid_spec=grid_spec,
        out_shape=jax.ShapeDtypeStruct((n_rows, D_MODEL), F32),
        compiler_params=_cparams("arbitrary"),
        name="experts",
    )(block_e, n_used, xbuf, w_gate, w_up, w_down)


def _combine_kernel(cur_ref, nxt_ref, gw_ref, x1_ref, p_ref, ybuf_ref, wpp_ref, wpg_ref,
                    g2_ref, b2_ref, gp_ref, o_ref, y_scr, sems, *, n_tiles):
    tm = x1_ref.shape[0]
    i = pl.program_id(0)
    slot = i & 1

    def gather(src_ref, s):
        for t in range(tm):
            for k in range(2):
                src = src_ref[0, 0, 2 * t + k]
                _row_copy(ybuf_ref.at[pl.ds(src, 1), :], y_scr.at[s, k, pl.ds(t, 1), :], sems.at[s]).start()

    @pl.when(i == 0)
    def _():
        gather(cur_ref, 0)

    if n_tiles > 1:
        @pl.when(i < n_tiles - 1)
        def _():
            gather(nxt_ref, 1 - slot)

    for _ in range(2 * tm):
        _row_copy(ybuf_ref.at[pl.ds(0, 1), :], y_scr.at[slot, 0, pl.ds(0, 1), :], sems.at[slot]).wait()

    gw = gw_ref[...]
    y = gw[:, 0:1] * y_scr[slot, 0] + gw[:, 1:2] * y_scr[slot, 1]
    x2 = _layernorm(ALPHA * x1_ref[...] + y, g2_ref[...], b2_ref[...])
    e = _rms(_dot(p_ref[...].astype(BF16), wpp_ref[...]), gp_ref[...])
    o_ref[...] = x2 + jax.nn.sigmoid(_dot(x2.astype(BF16), wpg_ref[...])) * e


def _combine(dst, gw, x1, p_l, ybuf, wpp, wpg, g2, b2, gp):
    n = x1.shape[0]
    tm = TOKEN_TILE
    nt = n // tm
    row = lambda i: (i, 0)
    fixed = lambda i: (0, 0)
    return pl.pallas_call(
        functools.partial(_combine_kernel, n_tiles=nt),
        grid=(nt,),
        in_specs=[
            pl.BlockSpec((1, 1, 2 * tm), lambda i: (i, 0, 0), memory_space=pltpu.SMEM),
            pl.BlockSpec((1, 1, 2 * tm), lambda i: (jnp.minimum(i + 1, nt - 1), 0, 0), memory_space=pltpu.SMEM),
            pl.BlockSpec((tm, LANES), row),
            pl.BlockSpec((tm, D_MODEL), row),
            pl.BlockSpec((tm, D_PLE), row),
            pl.BlockSpec(memory_space=pl.ANY),
            pl.BlockSpec(wpp.shape, fixed),
            pl.BlockSpec(wpg.shape, fixed),
            pl.BlockSpec(g2.shape, fixed),
            pl.BlockSpec(b2.shape, fixed),
            pl.BlockSpec(gp.shape, fixed),
        ],
        out_specs=pl.BlockSpec((tm, D_MODEL), row),
        scratch_shapes=[pltpu.VMEM((2, 2, tm, D_MODEL), F32), pltpu.SemaphoreType.DMA((2,))],
        out_shape=jax.ShapeDtypeStruct((n, D_MODEL), F32),
        compiler_params=_cparams("arbitrary"),
        name="combine",
    )(dst, dst, gw, x1, p_l, ybuf, wpp, wpg, g2, b2, gp)


def _rope_tables(pos):
    half = HD_A // 2
    inv = ROPE_THETA ** (-jnp.arange(half, dtype=F32) / half)
    ang = pos.astype(F32)[:, None] * inv[None, :]
    cos, sin = jnp.cos(ang), jnp.sin(ang)
    reps = LANES // HD_A
    return (jnp.tile(jnp.concatenate([cos, cos], axis=1), (1, reps)),
            jnp.tile(jnp.concatenate([-sin, sin], axis=1), (1, reps)))


def _moe_plan(counts, n_tokens):
    blk = EXPERT_BLOCK
    nb = (2 * n_tokens) // blk + N_EXPERTS
    padded = (counts + blk - 1) // blk * blk
    e = jnp.arange(N_EXPERTS, dtype=I32)
    pends = jnp.sum(jnp.where(e[None, :] <= e[:, None], padded[None, :], 0), axis=1)
    pstart = (pends - padded).astype(I32)
    first_row = jnp.arange(nb, dtype=I32) * blk
    block_e = jnp.minimum(jnp.sum(pends[None, :] <= first_row[:, None], axis=1), N_EXPERTS - 1).astype(I32)
    n_used = (pends[-1:] // blk).astype(I32)
    return pstart, block_e, n_used, nb * blk


def _mixers(x, lw, rope, attend, s0, cnt0, *, batch, seq, chunk, groups, kv_prev):
    cos_t, sin_t = rope
    qa, ka, qb, kb, vb, gb, lg, vt, k4, v4 = _inproj(
        x, lw["w_main"], lw["w_vt"], lw["w_glr"], lw["w_gk2"], lw["b_gk2"], cos_t, sin_t,
        groups=groups, kv_prev=kv_prev)
    oa = attend(qa, ka, vt, k4, v4)
    ob, s_out = _gla(qb, kb, vb, lg, s0, lw["gla_g"], batch=batch, seq=seq, chunk=chunk)
    x1, gw, idx, cnt = _mix(x, oa, ob, gb, lw["w_o"], lw["ln1_g"], lw["ln1_b"], lw["w_r"], lw["b_r"], cnt0)
    return (x1, gw, idx), cnt, k4, v4, s_out


def _slots(idx, pstart):
    eid, pos = idx[:, 0:2], idx[:, 2:4]
    first = jnp.sum(jnp.where(eid[:, :, None] == jnp.arange(N_EXPERTS, dtype=I32), pstart, 0), axis=-1)
    return (first + pos).reshape(idx.shape[0] // TOKEN_TILE, 1, 2 * TOKEN_TILE)


def _moe_and_embed(routed, cnt, p_ls, li, lw):
    n_tokens = sum(r[0].shape[0] for r in routed)
    pstart, block_e, n_used, n_rows = _moe_plan(cnt[0, :N_EXPERTS], n_tokens)
    dsts = [_slots(idx, pstart) for _, _, idx in routed]
    xbuf = jnp.zeros((n_rows, D_MODEL), F32)
    for (x1, _, _), dst in zip(routed, dsts):
        xbuf = _dispatch(dst, x1, xbuf)
    ybuf = _experts(block_e, n_used, xbuf, lw["w_gate"], lw["w_up"], lw["w_down"], li=li)
    return [_combine(dst, gw, x1, p_l, ybuf, lw["w_pp"], lw["w_pg"], lw["ln2_g"], lw["ln2_b"], lw["ple_g"])
            for (x1, gw, _), dst, p_l in zip(routed, dsts, p_ls)]


def kernel(x_prompt, x_sample, cache_k, cache_v, state_gla, page_table, p_prompt, p_sample, w_in, w_gk2, b_gk2, lam_q1, lam_k1, lam_q2, lam_k2, diff_norm_g, gla_norm_g, w_o, ln1_g, ln1_b, w_r1, b_r1, w_r2, b_r2, w_gate, w_up, w_down, ln2_g, ln2_b, w_ple_gate, w_ple_proj, ple_norm_g):
    bp, tp, _ = x_prompt.shape
    bs, ts, _ = x_sample.shape
    past = page_table.shape[1] * PAGE_SIZE
    rope_p = _rope_tables(jnp.arange(tp))
    rope_s = tuple(jnp.tile(t, (TOKEN_TILE // ts, 1)) for t in _rope_tables(past + jnp.arange(ts)))

    row2 = lambda a: a.reshape(1, -1)
    yp = x_prompt.reshape(bp * tp, D_MODEL)
    ys = x_sample.reshape(bs * ts, D_MODEL)
    s0_p = jnp.zeros((bp, H_B, DK_B, DV_B), F32)
    outs = {k: [] for k in ("sp", "ks", "vs", "ss")}
    kv_p = [None, None]
    for li in range(DEPTH):
        lw = {
            "w_main": w_in[li, :, :_OFF_GLR].astype(BF16),
            "w_vt": jnp.transpose(w_in[li, :, _OFF_VA:_OFF_VA + W_A]).astype(BF16),
            "w_glr": jnp.pad(w_in[li, :, _OFF_GLR:], ((0, 0), (0, LANES - GATE_RANK))).astype(BF16),
            "w_gk2": jnp.pad(w_gk2[li], ((0, LANES - GATE_RANK), (0, 0))),
            "b_gk2": row2(b_gk2[li]),
            "gla_g": row2(gla_norm_g[li]),
            "w_o": w_o[li].astype(BF16),
            "ln1_g": row2(ln1_g[li]), "ln1_b": row2(ln1_b[li]),
            "w_r": jnp.pad(jnp.concatenate([w_r2[li], w_r1[li]], axis=1),
                           ((0, 0), (0, LANES - N_EXPERTS - N_GROUPS))),
            "b_r": row2(jnp.pad(jnp.concatenate([b_r2[li], b_r1[li]]), (0, LANES - N_EXPERTS - N_GROUPS))),
            "w_gate": w_gate, "w_up": w_up, "w_down": w_down,
            "w_pp": w_ple_proj[li].astype(BF16),
            "w_pg": w_ple_gate[li].astype(BF16),
            "ln2_g": row2(ln2_g[li]), "ln2_b": row2(ln2_b[li]),
            "ple_g": row2(ple_norm_g[li]),
        }
        lamq = jnp.stack([lam_q1[li], lam_q2[li]])
        lamk = jnp.stack([lam_k1[li], lam_k2[li]])
        dg = row2(diff_norm_g[li])

        def attend_p(qa, ka, vt, k4, v4, lamq=lamq, lamk=lamk, dg=dg, li=li):
            del k4, v4
            return _prompt_attention(qa, ka, vt, lamq, lamk, dg, li=li, batch=bp, seq=tp)

        routed_p, cnt, kv_p[0], kv_p[1], s = _mixers(
            yp, lw, rope_p, attend_p, s0_p, jnp.zeros((1, LANES), I32),
            batch=bp, seq=tp, chunk=GLA_CHUNK, groups=bp, kv_prev=None if li == 0 else tuple(kv_p))
        outs["sp"].append(s)

        def attend_s(qa, ka, vt, k4, v4, lamq=lamq, lamk=lamk, dg=dg, li=li):
            del ka, vt
            return _sample_attention(qa, k4.reshape(-1, DKV_A), v4.reshape(-1, DKV_A), cache_k, cache_v,
                                     page_table, lamq, lamk, dg, li=li, batch=bs, ts=ts)

        routed_s, cnt, k4, v4, s = _mixers(ys, lw, rope_s, attend_s, state_gla[:, li], cnt,
                                           batch=bs, seq=ts, chunk=ts, groups=1, kv_prev=None)
        outs["ks"].append(k4.reshape(bs, ts, H_A, DKV_A))
        outs["vs"].append(v4.reshape(bs, ts, H_A, DKV_A))
        outs["ss"].append(s)

        yp, ys = _moe_and_embed(
            [routed_p, routed_s], cnt,
            [p_prompt[li].reshape(bp * tp, D_PLE), p_sample[li].reshape(bs * ts, D_PLE)], li, lw)

    stack = lambda k: jnp.stack(outs[k], axis=1)
    return (yp.reshape(bp, tp, D_MODEL), ys.reshape(bs, ts, D_MODEL),
            kv_p[0].reshape(bp, DEPTH, tp, H_A, DKV_A), kv_p[1].reshape(bp, DEPTH, tp, H_A, DKV_A),
            stack("sp"), stack("ks"), stack("vs"), stack("ss"))
```

```python
import functools
import math

import jax
import jax.numpy as jnp
from jax import lax
from jax.experimental import pallas as pl
from jax.experimental.pallas import tpu as pltpu

F32 = jnp.float32
BF16 = jnp.bfloat16
I32 = jnp.int32
HI = lax.Precision.HIGHEST

D_MODEL = 1024
DEPTH = 2
PAGE_SIZE = 128
D_PLE = 256
HD_A = 64
DKV_A = 2 * HD_A
W_A = D_MODEL // 2
H_A = W_A // DKV_A
W_B = D_MODEL - W_A
H_B = 4
DV_B = W_B // H_B
DK_B = DV_B // 2
GATE_RANK = 16
GATE_NORM = 16.0
GLA_CHUNK = 64
N_GROUPS = 4
EXP_PER_GROUP = 8
N_EXPERTS = N_GROUPS * EXP_PER_GROUP
D_EXPERT = D_MODEL // 2
ROPE_THETA = 10000.0
ALPHA = (2 * DEPTH) ** 0.25
EPS = 1e-5
LOG2E = math.log2(math.e)

LANES = 128
VMEM_LIMIT = 48 * 1024 * 1024

TOKEN_TILE = 256
ATTN_TILE = 256
PAGES_PER_STEP = 8
PAGE_SLOTS = 3
PAGE_GROUP = 4
GLA_TILE = 256
EXPERT_BLOCK = 256
MIX_ROWS = 128

_OFF_QA, _OFF_KA, _OFF_VA = 0, W_A, 2 * W_A
_OFF_QB = 3 * W_A
_OFF_KB = _OFF_QB + H_B * DK_B
_OFF_VB = _OFF_KB + H_B * DK_B
_OFF_GB = _OFF_VB + W_B
_OFF_GLR = _OFF_GB + W_B

_NT = (((1,), (1,)), ((), ()))
_TN = (((0,), (0,)), ((), ()))


def _cparams(*sem):
    return pltpu.CompilerParams(dimension_semantics=sem, vmem_limit_bytes=VMEM_LIMIT)


def _dot(a, b, precision=None):
    return jnp.dot(a, b, preferred_element_type=F32, precision=precision)


def _lam_init(li):
    return 0.8 - 0.6 * math.exp(-0.3 * li)


def _lam(lamq_ref, lamk_ref, li):
    s = jnp.sum(lamq_ref[...] * lamk_ref[...], axis=1, keepdims=True)
    e = jnp.exp(s)
    return e[0:1, :] - e[1:2, :] + _lam_init(li)


def _rms(x, g):
    return x * lax.rsqrt(jnp.mean(x * x, axis=-1, keepdims=True) + EPS) * g


def _layernorm(x, g, b):
    mu = jnp.mean(x, axis=-1, keepdims=True)
    xc = x - mu
    var = jnp.mean(xc * xc, axis=-1, keepdims=True)
    return xc * lax.rsqrt(var + EPS) * g + b


def _inproj_kernel(x_ref, w_ref, wvt_ref, wglr_ref, wgk2_ref, bgk2_ref, cos_ref, sin_ref, *refs):
    qa_ref, ka_ref, qb_ref, kb_ref, vb_ref, gb_ref, lg_ref, vt_ref, k4_ref, v4_ref = refs[-10:]
    tm = x_ref.shape[0]
    layer = k4_ref.shape[0] - 1
    if len(refs) > 10:
        k4_ref[0:layer] = refs[0][...]
        v4_ref[0:layer] = refs[1][...]
    xb = x_ref[...].astype(BF16)
    vt_ref[...] = lax.dot_general(wvt_ref[...], xb, _NT, preferred_element_type=F32).astype(BF16)
    glr = _dot(xb, wglr_ref[...])
    z = _dot(glr, wgk2_ref[...], HI) + bgk2_ref[...]
    lg_ref[...] = (jnp.minimum(z, 0.0) - jnp.log1p(jnp.exp(-jnp.abs(z)))) * (1.0 / GATE_NORM)
    cos = cos_ref[...]
    sin = sin_ref[...]
    lane = lax.broadcasted_iota(I32, cos.shape, 1)
    first_half = (lane & (HD_A // 2)) == 0

    def rope(h):
        partner = jnp.where(first_half, pltpu.roll(h, LANES - HD_A // 2, 1), pltpu.roll(h, HD_A // 2, 1))
        return h * cos + partner * sin

    for c in range(W_A // LANES):
        lo = c * LANES
        hq = _dot(xb, w_ref[:, _OFF_QA + lo:_OFF_QA + lo + LANES])
        qa_ref[:, lo:lo + LANES] = rope(hq) * (HD_A ** -0.5 * LOG2E)
        hk = rope(_dot(xb, w_ref[:, _OFF_KA + lo:_OFF_KA + lo + LANES]))
        ka_ref[:, lo:lo + LANES] = hk
        k4_ref[layer, pl.ds(c, tm, stride=H_A), :] = hk
        v4_ref[layer, pl.ds(c, tm, stride=H_A), :] = _dot(xb, w_ref[:, _OFF_VA + lo:_OFF_VA + lo + LANES])
    qb_ref[...] = _dot(xb, w_ref[:, _OFF_QB:_OFF_KB]) * (DK_B ** -0.5)
    kb_ref[...] = _dot(xb, w_ref[:, _OFF_KB:_OFF_VB])
    vb_ref[...] = _dot(xb, w_ref[:, _OFF_VB:_OFF_GB])
    gb_ref[...] = _dot(xb, w_ref[:, _OFF_GB:_OFF_GLR])


def _inproj(x, w_main, w_vt, w_glr, w_gk2p, b_gk2, cos_t, sin_t, *, groups, kv_prev=None):
    n = x.shape[0]
    tm = TOKEN_TILE
    ntab = cos_t.shape[0] // tm
    nq = n // tm // groups
    kv_idx = lambda i: (i // nq, 0, i % nq, 0)
    layers = 1 if kv_prev is None else kv_prev[0].shape[1] + 1
    row = lambda i: (i, 0)
    fixed = lambda i: (0, 0)
    tab = lambda i: (i % ntab, 0)
    widths = (W_A, W_A, H_B * DK_B, H_B * DK_B, W_B, W_B, H_B * DK_B)
    operands = [x, w_main, w_vt, w_glr, w_gk2p, b_gk2, cos_t, sin_t]
    in_specs = [
        pl.BlockSpec((tm, D_MODEL), row),
        pl.BlockSpec(w_main.shape, fixed),
        pl.BlockSpec(w_vt.shape, fixed),
        pl.BlockSpec(w_glr.shape, fixed),
        pl.BlockSpec(w_gk2p.shape, fixed),
        pl.BlockSpec(b_gk2.shape, fixed),
        pl.BlockSpec((tm, LANES), tab),
        pl.BlockSpec((tm, LANES), tab),
    ]
    if kv_prev is not None:
        operands += list(kv_prev)
        in_specs += [pl.BlockSpec((None, layers - 1, tm * H_A, DKV_A), kv_idx)] * 2
    kv_shape = jax.ShapeDtypeStruct((groups, layers, nq * tm * H_A, DKV_A), F32)
    return pl.pallas_call(
        _inproj_kernel,
        grid=(n // tm,),
        in_specs=in_specs,
        out_specs=[pl.BlockSpec((tm, w), row) for w in widths]
        + [pl.BlockSpec((None, W_A, tm), lambda i: (i, 0, 0))]
        + [pl.BlockSpec((None, layers, tm * H_A, DKV_A), kv_idx)] * 2,
        out_shape=[jax.ShapeDtypeStruct((n, w), F32) for w in widths]
        + [jax.ShapeDtypeStruct((n // tm, W_A, tm), BF16), kv_shape, kv_shape],
        compiler_params=_cparams("parallel"),
        name="inproj",
    )(*operands)


def _pattn_kernel(lamq_ref, lamk_ref, g_ref, q_ref, k_ref, vt_ref, o_ref, acc_scr, sa_scr, sb_scr, *, li):
    tq = q_ref.shape[0]
    win = 2 * tq
    last_win = k_ref.shape[0] // win - 1
    qi = pl.program_id(2)
    n_pairs = lax.shift_right_logical(qi, 2)
    q = q_ref[...]
    lane = lax.broadcasted_iota(I32, q.shape, 1)
    qc = (jnp.where(lane < HD_A, q, 0.0).astype(BF16), jnp.where(lane >= HD_A, q, 0.0).astype(BF16))
    acc_scr[...] = jnp.zeros(acc_scr.shape, F32)

    def scores(w, s_scr):
        w = jnp.minimum(w, last_win)
        start = pl.multiple_of(w * win, win)
        kb = k_ref[pl.ds(start, win), :].astype(BF16)
        for c in range(2):
            s_scr[c] = lax.dot_general(kb, qc[c], _NT, preferred_element_type=F32)

    def consume(w, s_scr, stats, masked):
        vt0 = vt_ref[2 * w]
        vt1 = vt_ref[2 * w + 1]
        out = []
        for c in range(2):
            m_prev, l_prev = stats[c]
            st = s_scr[c]
            if masked:
                key = lax.broadcasted_iota(I32, st.shape, 0)
                qry = lax.broadcasted_iota(I32, (1, tq), 1) + (qi * tq - w * win)
                st = jnp.where(key <= qry, st, -jnp.inf)
            m_new = jnp.maximum(m_prev, jnp.max(st, axis=0, keepdims=True))
            alpha = jnp.exp2(m_prev - m_new)
            p = jnp.exp2(st - m_new)
            l_new = alpha * l_prev + jnp.sum(p, axis=0, keepdims=True)
            p = p.astype(BF16)
            cols = slice(c * tq, (c + 1) * tq)
            acc_scr[:, cols] = alpha * acc_scr[:, cols] + (_dot(vt0, p[0:tq]) + _dot(vt1, p[tq:win]))
            out.append((m_new, l_new))
        return tuple(out)

    def pair(i, stats):
        scores(2 * i + 1, sb_scr)
        stats = consume(2 * i, sa_scr, stats, False)
        scores(2 * i + 2, sa_scr)
        return consume(2 * i + 1, sb_scr, stats, False)

    def last_pair(i, stats):
        def both(s):
            scores(2 * i + 1, sb_scr)
            s = consume(2 * i, sa_scr, s, True)
            return consume(2 * i + 1, sb_scr, s, True)

        def first_only(s):
            return consume(2 * i, sa_scr, s, True)

        reaches_second = (qi & 3) >= 2
        return lax.cond(reaches_second, both, first_only, stats)

    scores(0, sa_scr)
    stat = (jnp.full((1, tq), -jnp.inf, F32), jnp.zeros((1, tq), F32))
    stats = lax.fori_loop(0, n_pairs, pair, (stat, stat))
    (_, l1), (_, l2) = last_pair(n_pairs, stats)
    out_t = acc_scr[:, 0:tq] / l1 - _lam(lamq_ref, lamk_ref, li) * (acc_scr[:, tq:2 * tq] / l2)
    o_ref[...] = _rms(jnp.transpose(out_t), g_ref[...]) * (1.0 - _lam_init(li))


def _prompt_attention(qa, ka, vt, lamq, lamk, g, *, li, batch, seq):
    tq = ATTN_TILE
    assert tq == TOKEN_TILE and seq % (4 * tq) == 0
    nq = seq // tq
    small = lambda b, h, i: (0, 0)
    return pl.pallas_call(
        functools.partial(_pattn_kernel, li=li),
        grid=(batch, H_A, nq),
        in_specs=[
            pl.BlockSpec(lamq.shape, small),
            pl.BlockSpec(lamk.shape, small),
            pl.BlockSpec(g.shape, small),
            pl.BlockSpec((tq, DKV_A), lambda b, h, i: (b * nq + i, h)),
            pl.BlockSpec((seq, DKV_A), lambda b, h, i: (b, h)),
            pl.BlockSpec((nq, DKV_A, tq), lambda b, h, i: (b, h, 0)),
        ],
        out_specs=pl.BlockSpec((tq, DKV_A), lambda b, h, i: (b * nq + i, h)),
        out_shape=jax.ShapeDtypeStruct(qa.shape, F32),
        scratch_shapes=[pltpu.VMEM((DKV_A, 2 * tq), F32),
                        pltpu.VMEM((2, 2 * tq, tq), F32),
                        pltpu.VMEM((2, 2 * tq, tq), F32)],
        compiler_params=_cparams("parallel", "parallel", "arbitrary"),
        name="prompt_attn",
    )(lamq, lamk, g, qa, ka, vt)


def _sattn_kernel(pt_ref, lamq_ref, lamk_ref, g_ref, q_ref, kn_ref, vn_ref, ck_ref, cv_ref, o_ref,
                  qt_scr, m_scr, l_scr, acc_scr, kn_scr, vn_scr, kbuf, vbuf, sems, *, li, npages, ts):
    step = pl.program_id(1)
    steps_per_seq = pl.num_programs(1)
    n_steps = pl.num_programs(0) * steps_per_seq
    lin = pl.program_id(0) * steps_per_seq + step
    slot = lin % PAGE_SLOTS
    nrow = H_A * 2 * ts
    rows_per_page = PAGE_SIZE * H_A

    def page_copies(t, s):
        b = t // steps_per_seq
        g = t % steps_per_seq
        out = []
        for i in range(npages):
            page = pt_ref[b, g * npages + i]
            out.append(pltpu.make_async_copy(ck_ref.at[page, li], kbuf.at[s, i], sems.at[s]))
            out.append(pltpu.make_async_copy(cv_ref.at[page, li], vbuf.at[s, i], sems.at[s]))
        return out

    @pl.when(lin == 0)
    def _():
        for t in range(PAGE_SLOTS - 1):
            for cp in page_copies(t, t):
                cp.start()

    ahead = lin + (PAGE_SLOTS - 1)

    @pl.when(ahead < n_steps)
    def _():
        for cp in page_copies(ahead, ahead % PAGE_SLOTS):
            cp.start()

    def head_match(shape):
        row = lax.broadcasted_iota(I32, shape, 0)
        col = lax.broadcasted_iota(I32, shape, 1)
        return (col & (H_A - 1)) == (row // (2 * ts)), row, col

    def update(state, scores, values):
        m_prev, l_prev, acc = state
        m_new = m_prev
        for s in scores:
            m_new = jnp.maximum(m_new, jnp.max(s, axis=1, keepdims=True))
        alpha = jnp.exp2(m_prev - m_new)
        l_new = alpha * l_prev
        acc = alpha * acc
        for s, v_bf in zip(scores, values):
            p = jnp.exp2(s - m_new)
            l_new = l_new + jnp.sum(p, axis=1, keepdims=True)
            acc = acc + _dot(p.astype(BF16), v_bf)
        return m_new, l_new, acc

    @pl.when(step == 0)
    def _():
        q = q_ref[...]
        lane = lax.broadcasted_iota(I32, (ts, DKV_A), 1)
        blocks = []
        for h in range(H_A):
            qh = q[:, h * DKV_A:(h + 1) * DKV_A]
            blocks += [jnp.where(lane < HD_A, qh, 0.0), jnp.where(lane >= HD_A, qh, 0.0)]
        qt = jnp.concatenate(blocks, axis=0).astype(BF16)
        qt_scr[...] = qt
        kn_scr[...] = jnp.zeros(kn_scr.shape, F32)
        vn_scr[...] = jnp.zeros(vn_scr.shape, F32)
        kn_scr[0:ts * H_A, :] = kn_ref[...]
        vn_scr[0:ts * H_A, :] = vn_ref[...]
        s = lax.dot_general(qt, kn_scr[...].astype(BF16), _NT, preferred_element_type=F32)
        match, row, col = head_match(s.shape)
        ok = match & ((col // H_A) <= (row & (ts - 1))) & (col < ts * H_A)
        s = jnp.where(ok, s, -jnp.inf)
        init = (jnp.full((nrow, 1), -jnp.inf, F32), jnp.zeros((nrow, 1), F32), jnp.zeros((nrow, DKV_A), F32))
        m, l, acc = update(init, [s], [vn_scr[...].astype(BF16)])
        m_scr[...] = jnp.broadcast_to(m, m_scr.shape)
        l_scr[...] = jnp.broadcast_to(l, l_scr.shape)
        acc_scr[...] = acc

    for cp in page_copies(lin, slot):
        cp.wait()

    qt = qt_scr[...]
    match, _, _ = head_match((nrow, rows_per_page))
    scores = [jnp.where(match, lax.dot_general(qt, kbuf[slot, i].astype(BF16), _NT, preferred_element_type=F32),
                        -jnp.inf) for i in range(npages)]
    state = (m_scr[:, 0:1], l_scr[:, 0:1], acc_scr[...])
    for g0 in range(0, npages, PAGE_GROUP):
        state = update(state, scores[g0:g0 + PAGE_GROUP],
                       [vbuf[slot, i].astype(BF16) for i in range(g0, g0 + PAGE_GROUP)])
    m, l, acc = state
    m_scr[...] = jnp.broadcast_to(m, m_scr.shape)
    l_scr[...] = jnp.broadcast_to(l, l_scr.shape)
    acc_scr[...] = acc

    @pl.when(step == steps_per_seq - 1)
    def _():
        o = acc_scr[...] / l_scr[:, 0:1]
        lam = _lam(lamq_ref, lamk_ref, li)
        for h in range(H_A):
            r0 = h * 2 * ts
            out = o[r0:r0 + ts, :] - lam * o[r0 + ts:r0 + 2 * ts, :]
            o_ref[:, h * DKV_A:(h + 1) * DKV_A] = _rms(out, g_ref[...]) * (1.0 - _lam_init(li))


def _sample_attention(qa, kn, vn, cache_k, cache_v, page_table, lamq, lamk, g, *, li, batch, ts):
    npg = PAGES_PER_STEP
    n_pages = page_table.shape[1]
    assert n_pages % npg == 0 and npg % PAGE_GROUP == 0 and ts == 8 and ts * H_A <= PAGE_SIZE
    assert batch * (n_pages // npg) >= PAGE_SLOTS
    nrow = H_A * 2 * ts
    rows_per_page = PAGE_SIZE * H_A
    small = lambda b, s, pt: (0, 0)
    new = lambda b, s, pt: (b, 0)
    grid_spec = pltpu.PrefetchScalarGridSpec(
        num_scalar_prefetch=1,
        grid=(batch, n_pages // npg),
        in_specs=[
            pl.BlockSpec(lamq.shape, small),
            pl.BlockSpec(lamk.shape, small),
            pl.BlockSpec(g.shape, small),
            pl.BlockSpec((ts, W_A), new),
            pl.BlockSpec((ts * H_A, DKV_A), new),
            pl.BlockSpec((ts * H_A, DKV_A), new),
            pl.BlockSpec(memory_space=pl.ANY),
            pl.BlockSpec(memory_space=pl.ANY),
        ],
        out_specs=pl.BlockSpec((ts, W_A), new),
        scratch_shapes=[
            pltpu.VMEM((nrow, DKV_A), BF16),
            pltpu.VMEM((nrow, LANES), F32),
            pltpu.VMEM((nrow, LANES), F32),
            pltpu.VMEM((nrow, DKV_A), F32),
            pltpu.VMEM((PAGE_SIZE, DKV_A), F32),
            pltpu.VMEM((PAGE_SIZE, DKV_A), F32),
            pltpu.VMEM((PAGE_SLOTS, npg, rows_per_page, DKV_A), F32),
            pltpu.VMEM((PAGE_SLOTS, npg, rows_per_page, DKV_A), F32),
            pltpu.SemaphoreType.DMA((PAGE_SLOTS,)),
        ],
    )
    ck = cache_k.reshape(cache_k.shape[0], DEPTH, rows_per_page, DKV_A)
    cv = cache_v.reshape(cache_v.shape[0], DEPTH, rows_per_page, DKV_A)
    return pl.pallas_call(
        functools.partial(_sattn_kernel, li=li, npages=npg, ts=ts),
        grid_spec=grid_spec,
        out_shape=jax.ShapeDtypeStruct(qa.shape, F32),
        compiler_params=_cparams("arbitrary", "arbitrary"),
        name="sample_attn",
    )(page_table, lamq, lamk, g, qa, kn, vn, ck, cv)


def _gla_kernel(q_ref, k_ref, v_ref, lg_ref, s0_ref, g_ref, o_ref, sout_ref, s_scr, *, chunk):
    t = pl.program_id(1)

    @pl.when(t == 0)
    def _():
        s_scr[...] = s0_ref[...]

    tile = q_ref.shape[0]
    n_chunks = tile // chunk
    row = lax.broadcasted_iota(I32, (tile, tile), 0)
    col = lax.broadcasted_iota(I32, (tile, tile), 1)
    tri = ((col <= row) & ((row // chunk) == (col // chunk))).astype(F32)
    causal = (lax.broadcasted_iota(I32, (chunk, chunk), 1) <= lax.broadcasted_iota(I32, (chunk, chunk), 0))

    bc = _dot(tri, lg_ref[...], HI)
    bl = jnp.concatenate([jnp.broadcast_to(bc[(c + 1) * chunk - 1:(c + 1) * chunk, :], (chunk, H_B * DK_B))
                          for c in range(n_chunks)], axis=0)
    k_all = k_ref[...]
    qg_all = (q_ref[...] * jnp.exp(bc)).astype(BF16)
    kg_all = (k_all * jnp.exp(-bc)).astype(BF16)
    kd_all = (k_all * jnp.exp(bl - bc)).astype(BF16)
    v_all = v_ref[...].astype(BF16)
    intra, incr, decay = {}, {}, {}
    for c in range(n_chunks):
        rows = slice(c * chunk, (c + 1) * chunk)
        decay[c] = jnp.transpose(jnp.broadcast_to(jnp.exp(bl[c * chunk:c * chunk + 1, :]),
                                                  (DV_B, H_B * DK_B)))
        for h in range(H_B):
            dk = slice(h * DK_B, (h + 1) * DK_B)
            dv = slice(h * DV_B, (h + 1) * DV_B)
            a = lax.dot_general(qg_all[rows, dk], kg_all[rows, dk], _NT, preferred_element_type=F32)
            intra[c, h] = jnp.where(causal, a, 0.0).astype(BF16)
            incr[c, h] = lax.dot_general(kd_all[rows, dk], v_all[rows, dv], _TN, preferred_element_type=F32)

    for h in range(H_B):
        dk = slice(h * DK_B, (h + 1) * DK_B)
        dv = slice(h * DV_B, (h + 1) * DV_B)
        s_cur = s_scr[h]
        for c in range(n_chunks):
            rows = slice(c * chunk, (c + 1) * chunk)
            o = _dot(qg_all[rows, dk], s_cur.astype(BF16)) + _dot(intra[c, h], v_all[rows, dv])
            s_cur = decay[c][dk, :] * s_cur + incr[c, h]
            o_ref[rows, dv] = _rms(o, g_ref[...])
        s_scr[h] = s_cur

    @pl.when(t == pl.num_programs(1) - 1)
    def _():
        sout_ref[...] = s_scr[...]


def _gla(qb, kb, vb, lg, s0, g, *, batch, seq, chunk):
    tb = min(seq, GLA_TILE)
    assert seq % tb == 0 and tb % chunk == 0
    nt = seq // tb
    tok = lambda b, t: (b * nt + t, 0)
    st = lambda b, t: (b, 0, 0, 0)
    return pl.pallas_call(
        functools.partial(_gla_kernel, chunk=chunk),
        grid=(batch, nt),
        in_specs=[
            pl.BlockSpec((tb, H_B * DK_B), tok),
            pl.BlockSpec((tb, H_B * DK_B), tok),
            pl.BlockSpec((tb, W_B), tok),
            pl.BlockSpec((tb, H_B * DK_B), tok),
            pl.BlockSpec((None, H_B, DK_B, DV_B), st),
            pl.BlockSpec(g.shape, lambda b, t: (0, 0)),
        ],
        out_specs=[
            pl.BlockSpec((tb, W_B), tok),
            pl.BlockSpec((None, H_B, DK_B, DV_B), st),
        ],
        out_shape=[
            jax.ShapeDtypeStruct((batch * seq, W_B), F32),
            jax.ShapeDtypeStruct((batch, H_B, DK_B, DV_B), F32),
        ],
        scratch_shapes=[pltpu.VMEM((H_B, DK_B, DV_B), F32)],
        compiler_params=_cparams("parallel", "arbitrary"),
        name="gla",
    )(qb, kb, vb, lg, s0, g)


def _mix_kernel(x_ref, oa_ref, ob_ref, gb_ref, wo_ref, g1_ref, b1_ref, wr_ref, br_ref, cnt0_ref,
                x1_ref, gw_ref, idx_ref, cnt_ref, cnt_scr):
    i = pl.program_id(0)

    @pl.when(i == 0)
    def _():
        cnt_scr[...] = cnt0_ref[...].astype(F32)

    rc = MIX_ROWS
    lane = lax.broadcasted_iota(I32, (rc, LANES), 1)
    lane_f = lane.astype(F32)
    big = float(LANES)
    neg = -jnp.inf
    r = lax.broadcasted_iota(I32, (rc, rc), 0)
    c = lax.broadcasted_iota(I32, (rc, rc), 1)
    earlier = jnp.where(c < r, 1.0, 0.0).astype(BF16)

    def first_argmax(v, vmax):
        return jnp.min(jnp.where(v == vmax, lane_f, big), axis=1, keepdims=True)

    chunks = [slice(ch * rc, (ch + 1) * rc) for ch in range(x_ref.shape[0] // rc)]
    x1s = []
    for rows in chunks:
        gb = gb_ref[rows, :]
        obg = ob_ref[rows, :] * (gb * jax.nn.sigmoid(gb))
        mix = (_dot(oa_ref[rows, :].astype(BF16), wo_ref[0:W_A, :])
               + _dot(obg.astype(BF16), wo_ref[W_A:D_MODEL, :]))
        x1 = _layernorm(ALPHA * x_ref[rows, :] + mix, g1_ref[...], b1_ref[...])
        x1_ref[rows, :] = x1
        x1s.append(x1)
    all_logits = [_dot(x1, wr_ref[...], HI) + br_ref[...] for x1 in x1s]

    cnt = cnt_scr[...]
    for rows, logits in zip(chunks, all_logits):
        lg1 = jnp.where((lane >= N_EXPERTS) & (lane < N_EXPERTS + N_GROUPS), logits, neg)
        m1 = jnp.max(lg1, axis=1, keepdims=True)
        pg = 1.0 / jnp.sum(jnp.exp(lg1 - m1), axis=1, keepdims=True)
        grp = first_argmax(lg1, m1) - float(N_EXPERTS)
        in_grp = (lane_f >= grp * EXP_PER_GROUP) & (lane_f < (grp + 1.0) * EXP_PER_GROUP)
        lg2 = jnp.where(in_grp, logits, neg)
        v1 = jnp.max(lg2, axis=1, keepdims=True)
        i1 = first_argmax(lg2, v1)
        lg2b = jnp.where(lane_f == i1, neg, lg2)
        v2 = jnp.max(lg2b, axis=1, keepdims=True)
        i2 = first_argmax(lg2b, v2)
        t = jnp.exp(v2 - v1)
        w1 = pg / (1.0 + t)
        w2 = pg * t / (1.0 + t)

        hit1 = lane_f == i1
        hit2 = lane_f == i2
        sel = jnp.where(hit1 | hit2, 1.0, 0.0)
        rank = _dot(earlier, sel.astype(BF16)) + cnt
        pos1 = jnp.sum(jnp.where(hit1, rank, 0.0), axis=1, keepdims=True)
        pos2 = jnp.sum(jnp.where(hit2, rank, 0.0), axis=1, keepdims=True)
        cnt = cnt + jnp.sum(sel, axis=0, keepdims=True)

        gw_ref[rows, :] = jnp.where(lane == 0, w1, jnp.where(lane == 1, w2, 0.0))
        idx_f = jnp.where(lane == 0, i1, jnp.where(lane == 1, i2, jnp.where(lane == 2, pos1,
                          jnp.where(lane == 3, pos2, 0.0))))
        idx_ref[rows, :] = idx_f.astype(I32)
    cnt_scr[...] = cnt
    cnt_ref[...] = cnt.astype(I32)


def _mix(x, oa, ob, gb, wo, g1, b1, wr, br, cnt0):
    n = x.shape[0]
    tm = TOKEN_TILE
    row = lambda i: (i, 0)
    fixed = lambda i: (0, 0)
    return pl.pallas_call(
        _mix_kernel,
        grid=(n // tm,),
        in_specs=[
            pl.BlockSpec((tm, D_MODEL), row),
            pl.BlockSpec((tm, W_A), row),
            pl.BlockSpec((tm, W_B), row),
            pl.BlockSpec((tm, W_B), row),
            pl.BlockSpec(wo.shape, fixed),
            pl.BlockSpec(g1.shape, fixed),
            pl.BlockSpec(b1.shape, fixed),
            pl.BlockSpec(wr.shape, fixed),
            pl.BlockSpec(br.shape, fixed),
            pl.BlockSpec((1, LANES), fixed),
        ],
        out_specs=[
            pl.BlockSpec((tm, D_MODEL), row),
            pl.BlockSpec((tm, LANES), row),
            pl.BlockSpec((tm, LANES), row),
            pl.BlockSpec((1, LANES), fixed),
        ],
        out_shape=[
            jax.ShapeDtypeStruct((n, D_MODEL), F32),
            jax.ShapeDtypeStruct((n, LANES), F32),
            jax.ShapeDtypeStruct((n, LANES), I32),
            jax.ShapeDtypeStruct((1, LANES), I32),
        ],
        scratch_shapes=[pltpu.VMEM((1, LANES), F32)],
        compiler_params=_cparams("arbitrary"),
        name="mix_route",
    )(x, oa, ob, gb, wo, g1, b1, wr, br, cnt0)


def _row_copy(src, dst, sem):
    return pltpu.make_async_copy(src, dst, sem)


def _dispatch_kernel(dst_ref, x_ref, buf_in_ref, buf_ref, sem):
    del buf_in_ref
    tm = x_ref.shape[0]
    for t in range(tm):
        for k in range(2):
            dst = dst_ref[0, 0, 2 * t + k]
            _row_copy(x_ref.at[pl.ds(t, 1), :], buf_ref.at[pl.ds(dst, 1), :], sem).start()
    for _ in range(2 * tm):
        _row_copy(x_ref.at[pl.ds(0, 1), :], buf_ref.at[pl.ds(0, 1), :], sem).wait()


def _dispatch(dst, x1, buf):
    n = x1.shape[0]
    n_rows = buf.shape[0]
    tm = TOKEN_TILE
    return pl.pallas_call(
        _dispatch_kernel,
        grid=(n // tm,),
        in_specs=[
            pl.BlockSpec((1, 1, 2 * tm), lambda i: (i, 0, 0), memory_space=pltpu.SMEM),
            pl.BlockSpec((tm, D_MODEL), lambda i: (i, 0)),
            pl.BlockSpec(memory_space=pl.ANY),
        ],
        out_specs=pl.BlockSpec(memory_space=pl.ANY),
        scratch_shapes=[pltpu.SemaphoreType.DMA(())],
        out_shape=jax.ShapeDtypeStruct((n_rows, D_MODEL), F32),
        input_output_aliases={2: 0},
        compiler_params=_cparams("arbitrary"),
        name="dispatch",
    )(dst, x1, buf)


def _expert_kernel(be_ref, nu_ref, x_ref, wg_ref, wu_ref, wd_ref, y_ref, wgb, wub, wdb):
    j = pl.program_id(0)

    @pl.when(j < nu_ref[0])
    def _():
        prev = be_ref[jnp.maximum(j - 1, 0)]

        @pl.when((j == 0) | (be_ref[j] != prev))
        def _():
            wgb[...] = wg_ref[...].astype(BF16)
            wub[...] = wu_ref[...].astype(BF16)
            wdb[...] = wd_ref[...].astype(BF16)

        half = x_ref.shape[0] // 2
        halves = (slice(0, half), slice(half, 2 * half))
        xs = [x_ref[r, :].astype(BF16) for r in halves]
        hg = [_dot(x, wgb[...]) for x in xs]
        hu = [_dot(x, wub[...]) for x in xs]
        for r, g, u in zip(halves, hg, hu):
            h = g * jax.nn.sigmoid(g) * u
            y_ref[r, :] = _dot(h.astype(BF16), wdb[...])

    @pl.when(j >= nu_ref[0])
    def _():
        y_ref[...] = jnp.zeros(y_ref.shape, F32)


def _experts(block_e, n_used, xbuf, w_gate, w_up, w_down, *, li):
    n_rows = xbuf.shape[0]
    blk = EXPERT_BLOCK
    nb = n_rows // blk

    def rows(j, be, nu):
        return (jnp.minimum(j, nu[0] - 1), 0)

    def wsel(j, be, nu):
        return (li, be[jnp.minimum(j, nu[0] - 1)], 0, 0)

    grid_spec = pltpu.PrefetchScalarGridSpec(
        num_scalar_prefetch=2,
        grid=(nb,),
        in_specs=[
            pl.BlockSpec((blk, D_MODEL), rows),
            pl.BlockSpec((None, None, D_MODEL, D_EXPERT), wsel),
            pl.BlockSpec((None, None, D_MODEL, D_EXPERT), wsel),
            pl.BlockSpec((None, None, D_EXPERT, D_MODEL), wsel),
        ],
        out_specs=pl.BlockSpec((blk, D_MODEL), lambda j, be, nu: (j, 0)),
        scratch_shapes=[
            pltpu.VMEM((D_MODEL, D_EXPERT), BF16),
            pltpu.VMEM((D_MODEL, D_EXPERT), BF16),
            pltpu.VMEM((D_EXPERT, D_MODEL), BF16),
        ],
    )
    return pl.pallas_call(
        _expert_kernel,
        grid_spec=grid_spec,
        out_shape=jax.ShapeDtypeStruct((n_rows, D_MODEL), F32),
        compiler_params=_cparams("arbitrary"),
        name="experts",
    )(block_e, n_used, xbuf, w_gate, w_up, w_down)


def _combine_kernel(cur_ref, nxt_ref, gw_ref, x1_ref, p_ref, ybuf_ref, wpp_ref, wpg_ref,
                    g2_ref, b2_ref, gp_ref, o_ref, y_scr, sems, *, n_tiles):
    tm = x1_ref.shape[0]
    i = pl.program_id(0)
    slot = i & 1

    def gather(src_ref, s):
        for t in range(tm):
            for k in range(2):
                src = src_ref[0, 0, 2 * t + k]
                _row_copy(ybuf_ref.at[pl.ds(src, 1), :], y_scr.at[s, k, pl.ds(t, 1), :], sems.at[s]).start()

    @pl.when(i == 0)
    def _():
        gather(cur_ref, 0)

    if n_tiles > 1:
        @pl.when(i < n_tiles - 1)
        def _():
            gather(nxt_ref, 1 - slot)

    for _ in range(2 * tm):
        _row_copy(ybuf_ref.at[pl.ds(0, 1), :], y_scr.at[slot, 0, pl.ds(0, 1), :], sems.at[slot]).wait()

    gw = gw_ref[...]
    y = gw[:, 0:1] * y_scr[slot, 0] + gw[:, 1:2] * y_scr[slot, 1]
    x2 = _layernorm(ALPHA * x1_ref[...] + y, g2_ref[...], b2_ref[...])
    e = _rms(_dot(p_ref[...].astype(BF16), wpp_ref[...]), gp_ref[...])
    o_ref[...] = x2 + jax.nn.sigmoid(_dot(x2.astype(BF16), wpg_ref[...])) * e


def _combine(dst, gw, x1, p_l, ybuf, wpp, wpg, g2, b2, gp):
    n = x1.shape[0]
    tm = TOKEN_TILE
    nt = n // tm
    row = lambda i: (i, 0)
    fixed = lambda i: (0, 0)
    return pl.pallas_call(
        functools.partial(_combine_kernel, n_tiles=nt),
        grid=(nt,),
        in_specs=[
            pl.BlockSpec((1, 1, 2 * tm), lambda i: (i, 0, 0), memory_space=pltpu.SMEM),
            pl.BlockSpec((1, 1, 2 * tm), lambda i: (jnp.minimum(i + 1, nt - 1), 0, 0), memory_space=pltpu.SMEM),
            pl.BlockSpec((tm, LANES), row),
            pl.BlockSpec((tm, D_MODEL), row),
            pl.BlockSpec((tm, D_PLE), row),
            pl.BlockSpec(memory_space=pl.ANY),
            pl.BlockSpec(wpp.shape, fixed),
            pl.BlockSpec(wpg.shape, fixed),
            pl.BlockSpec(g2.shape, fixed),
            pl.BlockSpec(b2.shape, fixed),
            pl.BlockSpec(gp.shape, fixed),
        ],
        out_specs=pl.BlockSpec((tm, D_MODEL), row),
        scratch_shapes=[pltpu.VMEM((2, 2, tm, D_MODEL), F32), pltpu.SemaphoreType.DMA((2,))],
        out_shape=jax.ShapeDtypeStruct((n, D_MODEL), F32),
        compiler_params=_cparams("arbitrary"),
        name="combine",
    )(dst, dst, gw, x1, p_l, ybuf, wpp, wpg, g2, b2, gp)


def _rope_tables(pos):
    half = HD_A // 2
    inv = ROPE_THETA ** (-jnp.arange(half, dtype=F32) / half)
    ang = pos.astype(F32)[:, None] * inv[None, :]
    cos, sin = jnp.cos(ang), jnp.sin(ang)
    reps = LANES // HD_A
    return (jnp.tile(jnp.concatenate([cos, cos], axis=1), (1, reps)),
            jnp.tile(jnp.concatenate([-sin, sin], axis=1), (1, reps)))


def _moe_plan(counts, n_tokens):
    blk = EXPERT_BLOCK
    nb = (2 * n_tokens) // blk + N_EXPERTS
    padded = (counts + blk - 1) // blk * blk
    e = jnp.arange(N_EXPERTS, dtype=I32)
    pends = jnp.sum(jnp.where(e[None, :] <= e[:, None], padded[None, :], 0), axis=1)
    pstart = (pends - padded).astype(I32)
    first_row = jnp.arange(nb, dtype=I32) * blk
    block_e = jnp.minimum(jnp.sum(pends[None, :] <= first_row[:, None], axis=1), N_EXPERTS - 1).astype(I32)
    n_used = (pends[-1:] // blk).astype(I32)
    return pstart, block_e, n_used, nb * blk


def _mixers(x, lw, rope, attend, s0, cnt0, *, batch, seq, chunk, groups, kv_prev):
    cos_t, sin_t = rope
    qa, ka, qb, kb, vb, gb, lg, vt, k4, v4 = _inproj(
        x, lw["w_main"], lw["w_vt"], lw["w_glr"], lw["w_gk2"], lw["b_gk2"], cos_t, sin_t,
        groups=groups, kv_prev=kv_prev)
    oa = attend(qa, ka, vt, k4, v4)
    ob, s_out = _gla(qb, kb, vb, lg, s0, lw["gla_g"], batch=batch, seq=seq, chunk=chunk)
    x1, gw, idx, cnt = _mix(x, oa, ob, gb, lw["w_o"], lw["ln1_g"], lw["ln1_b"], lw["w_r"], lw["b_r"], cnt0)
    return (x1, gw, idx), cnt, k4, v4, s_out


def _slots(idx, pstart):
    eid, pos = idx[:, 0:2], idx[:, 2:4]
    first = jnp.sum(jnp.where(eid[:, :, None] == jnp.arange(N_EXPERTS, dtype=I32), pstart, 0), axis=-1)
    return (first + pos).reshape(idx.shape[0] // TOKEN_TILE, 1, 2 * TOKEN_TILE)


def _moe_and_embed(routed, cnt, p_ls, li, lw):
    n_tokens = sum(r[0].shape[0] for r in routed)
    pstart, block_e, n_used, n_rows = _moe_plan(cnt[0, :N_EXPERTS], n_tokens)
    dsts = [_slots(idx, pstart) for _, _, idx in routed]
    xbuf = jnp.zeros((n_rows, D_MODEL), F32)
    for (x1, _, _), dst in zip(routed, dsts):
        xbuf = _dispatch(dst, x1, xbuf)
    ybuf = _experts(block_e, n_used, xbuf, lw["w_gate"], lw["w_up"], lw["w_down"], li=li)
    return [_combine(dst, gw, x1, p_l, ybuf, lw["w_pp"], lw["w_pg"], lw["ln2_g"], lw["ln2_b"], lw["ple_g"])
            for (x1, gw, _), dst, p_l in zip(routed, dsts, p_ls)]


def kernel(x_prompt, x_sample, cache_k, cache_v, state_gla, page_table, p_prompt, p_sample, w_in, w_gk2, b_gk2, lam_q1, lam_k1, lam_q2, lam_k2, diff_norm_g, gla_norm_g, w_o, ln1_g, ln1_b, w_r1, b_r1, w_r2, b_r2, w_gate, w_up, w_down, ln2_g, ln2_b, w_ple_gate, w_ple_proj, ple_norm_g):
    bp, tp, _ = x_prompt.shape
    bs, ts, _ = x_sample.shape
    past = page_table.shape[1] * PAGE_SIZE
    rope_p = _rope_tables(jnp.arange(tp))
    rope_s = tuple(jnp.tile(t, (TOKEN_TILE // ts, 1)) for t in _rope_tables(past + jnp.arange(ts)))

    row2 = lambda a: a.reshape(1, -1)
    yp = x_prompt.reshape(bp * tp, D_MODEL)
    ys = x_sample.reshape(bs * ts, D_MODEL)
    s0_p = jnp.zeros((bp, H_B, DK_B, DV_B), F32)
    outs = {k: [] for k in ("sp", "ks", "vs", "ss")}
    kv_p = [None, None]
    for li in range(DEPTH):
        lw = {
            "w_main": w_in[li, :, :_OFF_GLR].astype(BF16),
            "w_vt": jnp.transpose(w_in[li, :, _OFF_VA:_OFF_VA + W_A]).astype(BF16),
            "w_glr": jnp.pad(w_in[li, :, _OFF_GLR:], ((0, 0), (0, LANES - GATE_RANK))).astype(BF16),
            "w_gk2": jnp.pad(w_gk2[li], ((0, LANES - GATE_RANK), (0, 0))),
            "b_gk2": row2(b_gk2[li]),
            "gla_g": row2(gla_norm_g[li]),
            "w_o": w_o[li].astype(BF16),
            "ln1_g": row2(ln1_g[li]), "ln1_b": row2(ln1_b[li]),
            "w_r": jnp.pad(jnp.concatenate([w_r2[li], w_r1[li]], axis=1),
                           ((0, 0), (0, LANES - N_EXPERTS - N_GROUPS))),
            "b_r": row2(jnp.pad(jnp.concatenate([b_r2[li], b_r1[li]]), (0, LANES - N_EXPERTS - N_GROUPS))),
            "w_gate": w_gate, "w_up": w_up, "w_down": w_down,
            "w_pp": w_ple_proj[li].astype(BF16),
            "w_pg": w_ple_gate[li].astype(BF16),
            "ln2_g": row2(ln2_g[li]), "ln2_b": row2(ln2_b[li]),
            "ple_g": row2(ple_norm_g[li]),
        }
        lamq = jnp.stack([lam_q1[li], lam_q2[li]])
        lamk = jnp.stack([lam_k1[li], lam_k2[li]])
        dg = row2(diff_norm_g[li])

        def attend_p(qa, ka, vt, k4, v4, lamq=lamq, lamk=lamk, dg=dg, li=li):
            del k4, v4
            return _prompt_attention(qa, ka, vt, lamq, lamk, dg, li=li, batch=bp, seq=tp)

        routed_p, cnt, kv_p[0], kv_p[1], s = _mixers(
            yp, lw, rope_p, attend_p, s0_p, jnp.zeros((1, LANES), I32),
            batch=bp, seq=tp, chunk=GLA_CHUNK, groups=bp, kv_prev=None if li == 0 else tuple(kv_p))
        outs["sp"].append(s)

        def attend_s(qa, ka, vt, k4, v4, lamq=lamq, lamk=lamk, dg=dg, li=li):
            del ka, vt
            return _sample_attention(qa, k4.reshape(-1, DKV_A), v4.reshape(-1, DKV_A), cache_k, cache_v,
                                     page_table, lamq, lamk, dg, li=li, batch=bs, ts=ts)

        routed_s, cnt, k4, v4, s = _mixers(ys, lw, rope_s, attend_s, state_gla[:, li], cnt,
                                           batch=bs, seq=ts, chunk=ts, groups=1, kv_prev=None)
        outs["ks"].append(k4.reshape(bs, ts, H_A, DKV_A))
        outs["vs"].append(v4.reshape(bs, ts, H_A, DKV_A))
        outs["ss"].append(s)

        yp, ys = _moe_and_embed(
            [routed_p, routed_s], cnt,
            [p_prompt[li].reshape(bp * tp, D_PLE), p_sample[li].reshape(bs * ts, D_PLE)], li, lw)

    stack = lambda k: jnp.stack(outs[k], axis=1)
    return (yp.reshape(bp, tp, D_MODEL), ys.reshape(bs, ts, D_MODEL),
            kv_p[0].reshape(bp, DEPTH, tp, H_A, DKV_A), kv_p[1].reshape(bp, DEPTH, tp, H_A, DKV_A),
            stack("sp"), stack("ks"), stack("vs"), stack("ss"))
```
